```python
import math
import jax
import jax.numpy as jnp
from jax import lax
import numpy as np

D_MODEL = 1024
BATCH = 16
SEQ = 2048
DEPTH = 1
DEC_BATCH = 32
DEC_SEQ = 1
PAST_LEN = 16384
PAGE_SIZE = 128

HEAD_DIM = 64
NSA_HEADS = 8
NSA_KV_HEADS = 2
NSA_GROUP = NSA_HEADS // NSA_KV_HEADS
NSA_WIDTH = NSA_HEADS * HEAD_DIM
CMP_BLOCK = 32
CMP_STRIDE = 16
CMP_UNITS = CMP_BLOCK // CMP_STRIDE
SEL_BLOCK = 64
N_SELECT = 16
WINDOW = 512
DIFF_HEADS = 8
DIFF_QK_DIM = 32
DIFF_V_DIM = 2 * DIFF_QK_DIM
DIFF_WIDTH = DIFF_HEADS * DIFF_V_DIM
MIX_WIDTH = NSA_WIDTH + DIFF_WIDTH
KV_COLS = 2 * NSA_KV_HEADS * HEAD_DIM
DIFF_QK_COLS = DIFF_HEADS * 2 * DIFF_QK_DIM
PROJ_COLS = NSA_WIDTH + 3 * KV_COLS + 3 * NSA_HEADS + NSA_WIDTH + 2 * DIFF_QK_COLS + 2 * DIFF_WIDTH
ROPE_THETA = 500000.0
ROT_FRACTION = 4
Q_BLOCK = 128
NSA_Q_BLOCK = 64
NORM_EPS = 1e-6
NEG_INF = -1e30
FORCE_SCORE = 1e9

kernel_name = 'hymba_nsa_diffattn_decode_step'


def rms_norm(x, w):
    xf = x.astype(jnp.float32)
    y = xf * lax.rsqrt(jnp.mean(xf * xf, axis=-1, keepdims=True) + NORM_EPS)
    return (y * w.astype(jnp.float32)).astype(x.dtype)


def rope(x, pos):
    rot = x.shape[-1] // ROT_FRACTION
    half = rot // 2
    inv = ROPE_THETA ** (-jnp.arange(half, dtype=jnp.float32) / half)
    ang = pos.astype(jnp.float32)[:, None] * inv
    ang = ang.reshape(ang.shape[:1] + (1,) * (x.ndim - 3) + (half,))
    cos, sin = jnp.cos(ang), jnp.sin(ang)
    xf = x[..., :rot].astype(jnp.float32)
    x1, x2 = xf[..., :half], xf[..., half:]
    rotated = jnp.concatenate([x1 * cos - x2 * sin, x2 * cos + x1 * sin], axis=-1).astype(x.dtype)
    return jnp.concatenate([rotated, x[..., rot:]], axis=-1)


def masked_softmax(s, mask):
    s = jnp.where(mask, s, NEG_INF)
    e = jnp.where(mask, jnp.exp(s - jnp.max(s, axis=-1, keepdims=True)), 0.0)
    return e / jnp.maximum(jnp.sum(e, axis=-1, keepdims=True), 1e-30)


def split_proj(proj):
    sizes = (NSA_WIDTH, KV_COLS, KV_COLS, KV_COLS, 3 * NSA_HEADS, NSA_WIDTH,
             DIFF_QK_COLS, DIFF_QK_COLS, DIFF_WIDTH, DIFF_WIDTH)
    return jnp.split(proj, [int(c) for c in np.cumsum(sizes)[:-1]], axis=-1)


def pad_to_multiple(rows, size):
    pad = (-rows.shape[1]) % size
    return jnp.pad(rows, ((0, 0), (0, pad)) + ((0, 0),) * (rows.ndim - 2))


def project(x, pos, lp):
    B, T, _ = x.shape
    h = rms_norm(x, lp['norm_w'])
    q_n, kv_c, kv_s, kv_w, g_n, z_n, q_d, k_d, v_d, z_d = split_proj(h @ lp['w_in'])
    q_n = rope(rms_norm(q_n.reshape(B, T, NSA_HEADS, HEAD_DIM), lp['nsa_q_norm']), pos)
    q_n = q_n.reshape(B, T, NSA_KV_HEADS, NSA_GROUP, HEAD_DIM) * HEAD_DIM ** -0.5
    kv_c = kv_c.reshape(B, T, 2, NSA_KV_HEADS, HEAD_DIM)
    kv_s = kv_s.reshape(B, T, 2, NSA_KV_HEADS, HEAD_DIM)
    kv_s = jnp.stack([rope(rms_norm(kv_s[:, :, 0], lp['nsa_ks_norm']), pos), kv_s[:, :, 1]], axis=2)
    kv_w = kv_w.reshape(B, T, 2, NSA_KV_HEADS, HEAD_DIM)
    kv_w = jnp.stack([rope(rms_norm(kv_w[:, :, 0], lp['nsa_kw_norm']), pos), kv_w[:, :, 1]], axis=2)
    g_n = jax.nn.sigmoid(g_n.astype(jnp.float32)).reshape(B, T, 3, NSA_KV_HEADS, NSA_GROUP).astype(x.dtype)
    q_d = rope(rms_norm(q_d.reshape(B, T, DIFF_HEADS, 2, DIFF_QK_DIM), lp['diff_q_norm']), pos) * DIFF_QK_DIM ** -0.5
    k_d = rope(rms_norm(k_d.reshape(B, T, DIFF_HEADS, 2, DIFF_QK_DIM), lp['diff_k_norm']), pos)
    v_d = v_d.reshape(B, T, DIFF_HEADS, DIFF_V_DIM)
    return q_n, kv_c, kv_s, kv_w, g_n, z_n, q_d, k_d, v_d, z_d


def compress(rows, pos_emb, w1, w2):
    B, L, G, D = rows.shape
    n_units = L // CMP_STRIDE
    n_cmp = n_units - CMP_UNITS + 1
    units = rows[:, :n_units * CMP_STRIDE].reshape(B, n_units, CMP_STRIDE, G, D)
    blocks = jnp.concatenate([units[:, j:j + n_cmp] for j in range(CMP_UNITS)], axis=2)
    hid = jax.nn.silu(jnp.einsum('bnlgd,lde->bnge', blocks + pos_emb[:, None, :], w1))
    return jnp.einsum('bnge,ef->bngf', hid, w2)


def compressed_kv(kv_c, lp):
    kc = compress(kv_c[:, :, 0], lp['cmp_pos_k'], lp['cmp_w1_k'], lp['cmp_w2_k'])
    vc = compress(kv_c[:, :, 1], lp['cmp_pos_v'], lp['cmp_w1_v'], lp['cmp_w2_v'])
    cend = jnp.arange(kc.shape[1]) * CMP_STRIDE + CMP_BLOCK - 1
    kc = rope(rms_norm(kc, lp['nsa_kc_norm']), cend)
    return kc, vc, cend


def selection_overlap(n_cmp, n_sel):
    cs = np.arange(n_cmp)[:, None] * CMP_STRIDE
    ss = np.arange(n_sel)[None, :] * SEL_BLOCK
    ov = np.minimum(cs + CMP_BLOCK, ss + SEL_BLOCK) - np.maximum(cs, ss)
    return jnp.asarray(np.clip(ov, 0, None) / CMP_STRIDE, dtype=jnp.float32)


def nsa_cmp_sel(q, qpos, kc, vc, cend, k_sel, v_sel, overlap):
    B, Tq, G = q.shape[:3]
    D = q.shape[-1]
    s = jnp.einsum('btgrd,bngd->btgrn', q, kc).astype(jnp.float32)
    p_cmp = masked_softmax(s, (cend[None, :] <= qpos[:, None])[None, :, None, None, :])
    o_cmp = jnp.einsum('btgrn,bngd->btgrd', p_cmp.astype(vc.dtype), vc)
    n_sel = overlap.shape[1]
    imp = jnp.einsum('btgrn,nj->btgj', p_cmp, overlap)
    blk = jnp.arange(n_sel)[None, :]
    cur = (qpos // SEL_BLOCK)[:, None]
    valid = blk * SEL_BLOCK <= qpos[:, None]
    forced = (blk == 0) | (blk == cur) | (blk == cur - 1)
    imp = jnp.where(forced[None, :, None, :], FORCE_SCORE, jnp.where(valid[None, :, None, :], imp, NEG_INF))
    _, idx = lax.top_k(imp, min(N_SELECT, n_sel))
    kb = k_sel.reshape(B, n_sel, SEL_BLOCK, G, D).transpose(0, 3, 1, 2, 4)
    vb = v_sel.reshape(B, n_sel, SEL_BLOCK, G, D).transpose(0, 3, 1, 2, 4)
    bi = jnp.arange(B)[:, None, None, None]
    gi = jnp.arange(G)[None, None, :, None]
    n_keys = idx.shape[-1] * SEL_BLOCK
    kg = kb[bi, gi, idx].reshape(B, Tq, G, n_keys, D)
    vg = vb[bi, gi, idx].reshape(B, Tq, G, n_keys, D)
    kpos = (idx[..., None] * SEL_BLOCK + jnp.arange(SEL_BLOCK)).reshape(B, Tq, G, n_keys)
    s2 = jnp.einsum('btgrd,btgsd->btgrs', q, kg).astype(jnp.float32)
    p_sel = masked_softmax(s2, (kpos <= qpos[None, :, None, None])[:, :, :, None, :])
    o_sel = jnp.einsum('btgrs,btgsd->btgrd', p_sel.astype(vg.dtype), vg)
    return o_cmp, o_sel


def window_attend(q, k, v, qpos, kpos):
    s = jnp.einsum('btgrd,bsgd->btgrs', q, k).astype(jnp.float32)
    d = qpos[:, None] - kpos[None, :]
    mask = (d >= 0) & (d < WINDOW) & (kpos[None, :] >= 0)
    p = masked_softmax(s, mask[None, :, None, None, :])
    return jnp.einsum('btgrs,bsgd->btgrd', p.astype(v.dtype), v)


def diff_attend(q, k, v, qpos, kpos, lam):
    s = jnp.einsum('bqhcd,bkhcd->bhcqk', q, k).astype(jnp.float32)
    p = masked_softmax(s, kpos[None, :] <= qpos[:, None])
    a = p[:, :, 0] - lam * p[:, :, 1]
    return jnp.einsum('bhqk,bkhv->bqhv', a.astype(v.dtype), v)


def diff_lambda(lp, lam_init):
    f32 = jnp.float32
    return (jnp.exp(jnp.sum(lp['lambda_q1'].astype(f32) * lp['lambda_k1'].astype(f32)))
            - jnp.exp(jnp.sum(lp['lambda_q2'].astype(f32) * lp['lambda_k2'].astype(f32))) + lam_init)


def layer_output(x, o_branches, g_n, z_n, o_d, z_d, lam_init, lp):
    B, T = x.shape[:2]
    o_nsa = jnp.sum(o_branches * g_n[..., None], axis=2).reshape(B, T, NSA_WIDTH)
    o_d = (rms_norm(o_d, lp['diff_out_norm']) * (1.0 - lam_init)).reshape(B, T, DIFF_WIDTH)
    o = jnp.concatenate([o_nsa * jax.nn.silu(z_n), o_d * jax.nn.silu(z_d)], axis=-1)
    return x + o @ lp['w_out']


def prompt_layer(x, lp, lam_init):
    B, T, _ = x.shape
    pos = jnp.arange(T)
    q_n, kv_c, kv_s, kv_w, g_n, z_n, q_d, k_d, v_d, z_d = project(x, pos, lp)
    kc, vc, cend = compressed_kv(kv_c, lp)
    kv_sel = pad_to_multiple(kv_s, SEL_BLOCK)
    overlap = selection_overlap(kc.shape[1], kv_sel.shape[1] // SEL_BLOCK)
    kv_w_pad = jnp.pad(kv_w, ((0, 0), (WINDOW, 0), (0, 0), (0, 0), (0, 0)))

    def nsa_block(i):
        s0 = i * NSA_Q_BLOCK
        qb = lax.dynamic_slice_in_dim(q_n, s0, NSA_Q_BLOCK, axis=1)
        qpos = s0 + jnp.arange(NSA_Q_BLOCK)
        o_c, o_s = nsa_cmp_sel(qb, qpos, kc, vc, cend, kv_sel[:, :, 0], kv_sel[:, :, 1], overlap)
        kvw = lax.dynamic_slice_in_dim(kv_w_pad, s0, WINDOW + NSA_Q_BLOCK, axis=1)
        kpos = s0 - WINDOW + jnp.arange(WINDOW + NSA_Q_BLOCK)
        o_w = window_attend(qb, kvw[:, :, 0], kvw[:, :, 1], qpos, kpos)
        return jnp.stack([o_c, o_s, o_w], axis=2)

    o_n = lax.map(nsa_block, jnp.arange(T // NSA_Q_BLOCK))
    o_n = jnp.moveaxis(o_n, 0, 1).reshape((B, T) + o_n.shape[3:])

    lam = diff_lambda(lp, lam_init)

    def diff_block(i):
        s0 = i * Q_BLOCK
        qb = lax.dynamic_slice_in_dim(q_d, s0, Q_BLOCK, axis=1)
        return diff_attend(qb, k_d, v_d, s0 + jnp.arange(Q_BLOCK), pos, lam)

    o_d = lax.map(diff_block, jnp.arange(T // Q_BLOCK))
    o_d = jnp.moveaxis(o_d, 0, 1).reshape(B, T, DIFF_HEADS, DIFF_V_DIM)
    y = layer_output(x, o_n, g_n, z_n, o_d, z_d, lam_init, lp)
    return y, (kv_c, kv_s, kv_w[:, T - min(WINDOW, T):], k_d, v_d)


def sample_layer(x, cache_c, cache_s, cache_dk, cache_dv, win, page_table, lp, lam_init):
    DB, T, _ = x.shape
    past = page_table.shape[1] * cache_c.shape[1]
    pos = past + jnp.arange(T)
    q_n, kv_c, kv_s, kv_w, g_n, z_n, q_d, k_d, v_d, z_d = project(x, pos, lp)

    def gather(pool):
        return pool[page_table].reshape((DB, past) + pool.shape[2:])

    kc, vc, cend = compressed_kv(jnp.concatenate([gather(cache_c), kv_c], axis=1), lp)
    kv_sel = pad_to_multiple(jnp.concatenate([gather(cache_s), kv_s], axis=1), SEL_BLOCK)
    overlap = selection_overlap(kc.shape[1], kv_sel.shape[1] // SEL_BLOCK)
    o_c, o_s = nsa_cmp_sel(q_n, pos, kc, vc, cend, kv_sel[:, :, 0], kv_sel[:, :, 1], overlap)
    w_buf = win.shape[1]
    win_all = jnp.concatenate([win, kv_w], axis=1)
    kpos_w = past - w_buf + jnp.arange(w_buf + T)
    o_w = window_attend(q_n, win_all[:, :, 0], win_all[:, :, 1], pos, kpos_w)
    o_n = jnp.stack([o_c, o_s, o_w], axis=2)
    lam = diff_lambda(lp, lam_init)
    k_all = jnp.concatenate([gather(cache_dk), k_d], axis=1)
    v_all = jnp.concatenate([gather(cache_dv), v_d], axis=1)
    o_d = diff_attend(q_d, k_all, v_all, pos, jnp.arange(past + T), lam)
    y = layer_output(x, o_n, g_n, z_n, o_d, z_d, lam_init, lp)
    return y, (kv_c, kv_s, win_all[:, T:], k_d, v_d)


def setup_inputs(seed: int = 0) -> dict:
    key = jax.random.key(seed)
    k = jax.random.split(key, 32)
    n_pages = PAST_LEN // PAGE_SIZE
    n_used = DEC_BATCH * n_pages
    n_pool = n_used + n_used // 4
    w_buf = min(WINDOW, PAST_LEN)

    def normal(kk, shape, scale=1.0):
        return jax.random.normal(kk, shape, jnp.float32) * scale

    def gain(kk, dim):
        return 1.0 + normal(kk, (DEPTH, dim), 0.05)

    page_table = jax.random.permutation(k[0], n_pool)[:n_used].reshape(DEC_BATCH, n_pages).astype(jnp.int32)
    return {
        'x_prompt': normal(k[1], (BATCH, SEQ, D_MODEL)),
        'x_sample': normal(k[2], (DEC_BATCH, DEC_SEQ, D_MODEL)),
        'cache_nsa_cmp_kv': normal(k[3], (DEPTH, n_pool, PAGE_SIZE, 2, NSA_KV_HEADS, HEAD_DIM)),
        'cache_nsa_sel_kv': normal(k[4], (DEPTH, n_pool, PAGE_SIZE, 2, NSA_KV_HEADS, HEAD_DIM)),
        'cache_diff_k': normal(k[5], (DEPTH, n_pool, PAGE_SIZE, DIFF_HEADS, 2, DIFF_QK_DIM)),
        'cache_diff_v': normal(k[6], (DEPTH, n_pool, PAGE_SIZE, DIFF_HEADS, DIFF_V_DIM)),
        'state_nsa_win_kv': normal(k[7], (DEPTH, DEC_BATCH, w_buf, 2, NSA_KV_HEADS, HEAD_DIM)),
        'page_table': page_table,
        'norm_w': gain(k[8], D_MODEL),
        'w_in': normal(k[9], (DEPTH, D_MODEL, PROJ_COLS), D_MODEL ** -0.5),
        'nsa_q_norm': gain(k[10], HEAD_DIM),
        'nsa_kc_norm': gain(k[11], HEAD_DIM),
        'nsa_ks_norm': gain(k[12], HEAD_DIM),
        'nsa_kw_norm': gain(k[13], HEAD_DIM),
        'cmp_pos_k': normal(k[14], (DEPTH, CMP_BLOCK, HEAD_DIM), 0.1),
        'cmp_w1_k': normal(k[15], (DEPTH, CMP_BLOCK, HEAD_DIM, HEAD_DIM), (CMP_BLOCK * HEAD_DIM) ** -0.5),
        'cmp_w2_k': normal(k[16], (DEPTH, HEAD_DIM, HEAD_DIM), HEAD_DIM ** -0.5),
        'cmp_pos_v': normal(k[17], (DEPTH, CMP_BLOCK, HEAD_DIM), 0.1),
        'cmp_w1_v': normal(k[18], (DEPTH, CMP_BLOCK, HEAD_DIM, HEAD_DIM), (CMP_BLOCK * HEAD_DIM) ** -0.5),
        'cmp_w2_v': normal(k[19], (DEPTH, HEAD_DIM, HEAD_DIM), HEAD_DIM ** -0.5),
        'diff_q_norm': gain(k[20], DIFF_QK_DIM),
        'diff_k_norm': gain(k[21], DIFF_QK_DIM),
        'lambda_q1': normal(k[22], (DEPTH, DIFF_QK_DIM), 0.1),
        'lambda_k1': normal(k[23], (DEPTH, DIFF_QK_DIM), 0.1),
        'lambda_q2': normal(k[24], (DEPTH, DIFF_QK_DIM), 0.1),
        'lambda_k2': normal(k[25], (DEPTH, DIFF_QK_DIM), 0.1),
        'diff_out_norm': gain(k[26], DIFF_V_DIM),
        'w_out': normal(k[27], (DEPTH, MIX_WIDTH, D_MODEL), MIX_WIDTH ** -0.5),
    }


def reference(x_prompt, x_sample, cache_nsa_cmp_kv, cache_nsa_sel_kv, cache_diff_k, cache_diff_v,
              state_nsa_win_kv, page_table, norm_w, w_in, nsa_q_norm, nsa_kc_norm, nsa_ks_norm,
              nsa_kw_norm, cmp_pos_k, cmp_w1_k, cmp_w2_k, cmp_pos_v, cmp_w1_v, cmp_w2_v,
              diff_q_norm, diff_k_norm, lambda_q1, lambda_k1, lambda_q2, lambda_k2, diff_out_norm, w_out):
    y_p, y_s = x_prompt, x_sample
    p_states, s_states = [], []
    for layer in range(DEPTH):
        lp = {
            'norm_w': norm_w[layer], 'w_in': w_in[layer],
            'nsa_q_norm': nsa_q_norm[layer], 'nsa_kc_norm': nsa_kc_norm[layer],
            'nsa_ks_norm': nsa_ks_norm[layer], 'nsa_kw_norm': nsa_kw_norm[layer],
            'cmp_pos_k': cmp_pos_k[layer], 'cmp_w1_k': cmp_w1_k[layer], 'cmp_w2_k': cmp_w2_k[layer],
            'cmp_pos_v': cmp_pos_v[layer], 'cmp_w1_v': cmp_w1_v[layer], 'cmp_w2_v': cmp_w2_v[layer],
            'diff_q_norm': diff_q_norm[layer], 'diff_k_norm': diff_k_norm[layer],
            'lambda_q1': lambda_q1[layer], 'lambda_k1': lambda_k1[layer],
            'lambda_q2': lambda_q2[layer], 'lambda_k2': lambda_k2[layer],
            'diff_out_norm': diff_out_norm[layer], 'w_out': w_out[layer],
        }
        lam_init = 0.8 - 0.6 * math.exp(-0.3 * layer)
        y_p, ps = prompt_layer(y_p, lp, lam_init)
        y_s, ss = sample_layer(y_s, cache_nsa_cmp_kv[layer], cache_nsa_sel_kv[layer], cache_diff_k[layer],
                               cache_diff_v[layer], state_nsa_win_kv[layer], page_table, lp, lam_init)
        p_states.append(ps)
        s_states.append(ss)
    p_c, p_s, p_w, p_dk, p_dv = [jnp.stack(t, axis=0) for t in zip(*p_states)]
    s_c, s_s, s_w, s_dk, s_dv = [jnp.stack(t, axis=0) for t in zip(*s_states)]
    return (y_p, y_s, p_c, p_s, p_w, p_dk, p_dv, s_c, s_s, s_w, s_dk, s_dv)
```

```python
import functools
import math

import jax
import jax.numpy as jnp
import numpy as np
from jax import lax
from jax.experimental import pallas as pl
from jax.experimental.pallas import tpu as pltpu

HEAD_DIM = 64
NSA_HEADS = 8
NSA_KV_HEADS = 2
NSA_GROUP = NSA_HEADS // NSA_KV_HEADS
NSA_WIDTH = NSA_HEADS * HEAD_DIM
CMP_BLOCK = 32
CMP_STRIDE = 16
SEL_BLOCK = 64
N_SELECT = 16
WINDOW = 512
DIFF_HEADS = 8
DIFF_QK_DIM = 32
DIFF_V_DIM = 64
DIFF_WIDTH = DIFF_HEADS * DIFF_V_DIM
KV_COLS = 2 * NSA_KV_HEADS * HEAD_DIM
DIFF_QK_COLS = DIFF_HEADS * 2 * DIFF_QK_DIM
ROPE_THETA = 500000.0
ROT_FRACTION = 4
NORM_EPS = 1e-6
NEG_INF = -1e30
FORCE_SCORE = 1e9
EXCLUDED = -3e38

LANES = 128
KEY_CHUNK = 256
VMEM_LIMIT = 56 * 1024 * 1024

F32 = jnp.float32
BF16 = jnp.bfloat16

_SIZES = (NSA_WIDTH, KV_COLS, KV_COLS, KV_COLS, 3 * NSA_HEADS, NSA_WIDTH, DIFF_QK_COLS, DIFF_QK_COLS, DIFF_WIDTH,
          DIFF_WIDTH)
_OFFS = tuple(int(v) for v in np.concatenate([[0], np.cumsum(_SIZES)]))
STD_COLS = 4 * 512 + LANES
TRN_ROWS = 3 * KV_COLS + DIFF_QK_COLS + DIFF_WIDTH


def _cparams(sem):
    return pltpu.CompilerParams(dimension_semantics=sem, vmem_limit_bytes=VMEM_LIMIT)


def _sigmoid(x):
    return 1.0 / (1.0 + jnp.exp(-x))


def _dot(a, b):
    return jnp.dot(a, b, preferred_element_type=F32)


def _dot_nt(a, b):
    return lax.dot_general(a, b, (((1,), (1,)), ((), ())), preferred_element_type=F32)


def _rope_lanes(x, c, sa, sb, half):
    parts = []
    for k in range(x.shape[1] // LANES):
        xk = x[:, LANES * k:LANES * (k + 1)]
        parts.append(xk * c + pltpu.roll(xk, LANES - half, 1) * sa + pltpu.roll(xk, half, 1) * sb)
    return jnp.concatenate(parts, axis=1) if len(parts) > 1 else parts[0]


def _proj_kernel(x_ref, nw_ref, wstd_ref, wt_ref, g64_ref, g32_ref, qnw_ref, qdw_ref, cq_ref, saq_ref, sbq_ref,
                 cd_ref, sad_ref, sbd_ref, ksw_ref, kww_ref, kdw_ref, c8_ref, s8_ref, c4_ref, s4_ref,
                 qn_ref, qd_ref, zn_ref, zd_ref, gn_ref, kvc_ref, kvs_ref, kvw_ref, kd_ref, vd_ref):
    tm = x_ref.shape[0]
    x = x_ref[...]
    ms = jnp.mean(x * x, axis=-1, keepdims=True)
    h = ((x * lax.rsqrt(ms + NORM_EPS)) * nw_ref[...]).astype(BF16)

    std = _dot(h, wstd_ref[...])
    qn = std[:, 0:512]
    qn = qn * lax.rsqrt(_dot(qn * qn, g64_ref[...]) + NORM_EPS) * qnw_ref[...]
    qn = _rope_lanes(qn, cq_ref[...], saq_ref[...], sbq_ref[...], 8) * (HEAD_DIM ** -0.5)
    qn_ref[...] = qn.astype(BF16)
    qd = std[:, 512:1024]
    qd = qd * lax.rsqrt(_dot(qd * qd, g32_ref[...]) + NORM_EPS) * qdw_ref[...]
    qd = _rope_lanes(qd, cd_ref[...], sad_ref[...], sbd_ref[...], 4) * (DIFF_QK_DIM ** -0.5)
    qd_ref[...] = qd.astype(BF16)
    zn = std[:, 1024:1536]
    zn_ref[...] = (zn * _sigmoid(zn)).astype(BF16)
    zd = std[:, 1536:2048]
    zd_ref[...] = (zd * _sigmoid(zd)).astype(BF16)
    gn_ref[...] = _sigmoid(std[:, 2048:2048 + LANES])

    trn = _dot_nt(wt_ref[...], h)
    kvc_ref[0] = trn[0:256]

    c8 = c8_ref[...]
    s8 = s8_ref[...]

    def norm_rope_k64(xk, w_ref):
        x3 = xk.reshape(NSA_KV_HEADS, HEAD_DIM, tm)
        r = lax.rsqrt(jnp.mean(x3 * x3, axis=1, keepdims=True) + NORM_EPS)
        x3 = (x3 * r) * w_ref[...].reshape(1, HEAD_DIM, 1)
        x1 = x3[:, 0:8]
        x2 = x3[:, 8:16]
        return jnp.concatenate([x1 * c8 - x2 * s8, x2 * c8 + x1 * s8, x3[:, 16:]], axis=1).reshape(
            NSA_KV_HEADS * HEAD_DIM, tm)

    kvs_ref[0, 0:128] = norm_rope_k64(trn[256:384], ksw_ref)
    kvs_ref[0, 128:256] = trn[384:512]
    kvw_ref[0, 0:128] = norm_rope_k64(trn[512:640], kww_ref)
    kvw_ref[0, 128:256] = trn[640:768]

    kd = trn[768:1280].reshape(2 * DIFF_HEADS, DIFF_QK_DIM, tm)
    r = lax.rsqrt(jnp.mean(kd * kd, axis=1, keepdims=True) + NORM_EPS)
    kd = (kd * r) * kdw_ref[...].reshape(1, DIFF_QK_DIM, 1)
    rot = kd[:, 0:8].reshape(2 * DIFF_HEADS * 8, tm)
    first = lax.broadcasted_iota(jnp.int32, rot.shape, 0) % 8 < 4
    swapped = jnp.where(first, pltpu.roll(rot, rot.shape[0] - 4, 0), pltpu.roll(rot, 4, 0))
    rot = (rot.reshape(2 * DIFF_HEADS, 8, tm) * c4_ref[...] + swapped.reshape(2 * DIFF_HEADS, 8, tm) * s4_ref[...])
    kd_ref[0] = jnp.concatenate([rot, kd[:, 8:]], axis=1).reshape(DIFF_QK_COLS, tm)
    vd_ref[0] = trn[1280:1792]


def _rope_tables(pos):
    pos = pos.astype(F32)
    lane = np.arange(LANES)

    def std_tables(group, half):
        inv = ROPE_THETA ** (-jnp.arange(half, dtype=F32) / half)
        ang = pos[:, None] * inv
        cos, sin = jnp.cos(ang), jnp.sin(ang)
        m = lane % group
        cosl, sinl = cos[:, m % half], sin[:, m % half]
        c = jnp.where(m < 2 * half, cosl, 1.0)
        sa = jnp.where(m < half, -sinl, 0.0)
        sb = jnp.where((m >= half) & (m < 2 * half), sinl, 0.0)
        return (c, sa, sb), (cos.T, sin.T)

    tq, (c8, s8) = std_tables(HEAD_DIM, HEAD_DIM // ROT_FRACTION // 2)
    td, (c4, s4) = std_tables(DIFF_QK_DIM, DIFF_QK_DIM // ROT_FRACTION // 2)
    c4 = jnp.concatenate([c4, c4], axis=0)
    s4 = jnp.concatenate([-s4, s4], axis=0)
    return tq, td, (c8, s8), (c4, s4)


def _group_mean_matrix(width, group):
    idx = np.arange(width)
    return jnp.asarray((idx[:, None] // group == idx[None, :] // group) / group, dtype=F32)


def _project(x2, pos_tables, t_per_batch, prm, tm):
    n, d = x2.shape
    nb = n // t_per_batch
    nt = t_per_batch // tm
    (cq, saq, sbq), (cd, sad, sbd), (c8, s8), (c4, s4) = pos_tables

    def row(i):
        return (i, 0)

    def tab(i):
        return (i % nt, 0)

    def tabt(i):
        return (0, i % nt)

    def trn(i):
        return (i // nt, 0, i % nt)

    def const(i):
        return (0, 0)

    def full(a):
        return pl.BlockSpec(a.shape, const)

    consts1 = (prm['norm_w'], prm['w_std'], prm['w_trn'], prm['g64'], prm['g32'], prm['qnw'], prm['qdw'])
    consts2 = (prm['ksw'], prm['kww'], prm['kdw'])
    in_specs = ([pl.BlockSpec((tm, d), row)] + [full(a) for a in consts1]
                + [pl.BlockSpec((tm, LANES), tab)] * 6 + [full(a) for a in consts2]
                + [pl.BlockSpec((8, tm), tabt)] * 4)
    out_shape = ([jax.ShapeDtypeStruct((n, 512), BF16)] * 4 + [jax.ShapeDtypeStruct((n, LANES), F32)]
                 + [jax.ShapeDtypeStruct((nb, 256, t_per_batch), F32)] * 3
                 + [jax.ShapeDtypeStruct((nb, 512, t_per_batch), F32)] * 2)
    out_specs = ([pl.BlockSpec((tm, 512), row)] * 4 + [pl.BlockSpec((tm, LANES), row)]
                 + [pl.BlockSpec((1, 256, tm), trn)] * 3 + [pl.BlockSpec((1, 512, tm), trn)] * 2)
    return pl.pallas_call(
        _proj_kernel, grid=(n // tm,), in_specs=in_specs, out_specs=out_specs, out_shape=out_shape,
        compiler_params=_cparams(("parallel",)), name="proj",
    )(x2, *consts1, cq, saq, sbq, cd, sad, sbd, *consts2, c8, s8, c4, s4)


def _compress_units(xs_ref, nu, u0, nuc, w1_ref, posl_ref, w2_ref, c):
    lhs = jnp.concatenate(
        [xs_ref[c, pl.ds(u0 * CMP_STRIDE + l, nuc, stride=CMP_STRIDE), :] for l in range(CMP_STRIDE)],
        axis=1).astype(BF16)
    return _dot(lhs, w1_ref[c])


def _compress_finish(pq, posl_ref, w1_ref, w2_ref, c):
    nu = pq.shape[0]
    pb = _dot(posl_ref[c], w1_ref[c])
    bias = pb[0:1, 0:128] + pb[1:2, 128:256]
    hid = pq[:, 0:128] + pltpu.roll(pq[:, 128:256], nu - 1, 0) + bias
    hid = hid * _sigmoid(hid)
    return _dot(hid.astype(BF16), w2_ref[c])


def _kc_norm_rope(kc, g64_ref, kcw_ref, cc_ref, sac_ref, sbc_ref):
    kc = kc * lax.rsqrt(_dot(kc * kc, g64_ref[...]) + NORM_EPS) * kcw_ref[...]
    return _rope_lanes(kc, cc_ref[...], sac_ref[...], sbc_ref[...], 8)


def _cmp_prompt_kernel(kvc_ref, w1_ref, posl_ref, w2_ref, g64_ref, kcw_ref, cc_ref, sac_ref, sbc_ref,
                       kct_ref, vc_ref, xs_ref):
    t = kvc_ref.shape[2]
    nu = t // CMP_STRIDE
    for c in range(2):
        xs_ref[c] = kvc_ref[0, c * 128:(c + 1) * 128, :].T
    outs = []
    for c in range(2):
        pq = _compress_units(xs_ref, nu, 0, nu, w1_ref, posl_ref, w2_ref, c)
        outs.append(_compress_finish(pq, posl_ref, w1_ref, w2_ref, c))
    kc = _kc_norm_rope(outs[0], g64_ref, kcw_ref, cc_ref, sac_ref, sbc_ref)
    kct = kc.T
    for g in range(NSA_KV_HEADS):
        kct_ref[0, g] = kct[g * HEAD_DIM:(g + 1) * HEAD_DIM]
        vc_ref[0, g] = outs[1][:, g * HEAD_DIM:(g + 1) * HEAD_DIM]


def _compress_prompt(kvc_t, prm, cend_tables):
    nb, _, t = kvc_t.shape
    nu = t // CMP_STRIDE
    cc, sac, sbc = cend_tables

    def full(a):
        nd = a.ndim
        return pl.BlockSpec(a.shape, lambda b: (0,) * nd)

    consts = (prm['cmp_w1'], prm['cmp_posl'], prm['cmp_w2'], prm['g64_128'], prm['kcw'], cc, sac, sbc)
    return pl.pallas_call(
        _cmp_prompt_kernel, grid=(nb,),
        in_specs=[pl.BlockSpec((1, 256, t), lambda b: (b, 0, 0))] + [full(a) for a in consts],
        out_specs=[pl.BlockSpec((1, NSA_KV_HEADS, HEAD_DIM, nu), lambda b: (b, 0, 0, 0)),
                   pl.BlockSpec((1, NSA_KV_HEADS, nu, HEAD_DIM), lambda b: (b, 0, 0, 0))],
        out_shape=[jax.ShapeDtypeStruct((nb, NSA_KV_HEADS, HEAD_DIM, nu), F32),
                   jax.ShapeDtypeStruct((nb, NSA_KV_HEADS, nu, HEAD_DIM), F32)],
        scratch_shapes=[pltpu.VMEM((2, t, LANES), F32)],
        compiler_params=_cparams(("parallel",)), name="cmp_prompt",
    )(kvc_t, *consts)


def _softmax_rows(s, mask):
    sm = jnp.where(mask, s, NEG_INF)
    m = jnp.max(sm, axis=-1, keepdims=True)
    e = jnp.where(mask, jnp.exp(sm - m), 0.0)
    return e / jnp.maximum(jnp.sum(e, axis=-1, keepdims=True), 1e-30)


def _attend_chunks(q, kt_ref, vt_ref, c_lo, c_hi, mask_fn, s_scr, m_scr, l_scr, acc_scr):
    rows = q.shape[0]
    m_scr[...] = jnp.full((rows, LANES), NEG_INF, F32)

    def scores(c, carry):
        off = pl.multiple_of(c * KEY_CHUNK, KEY_CHUNK)
        loc = pl.multiple_of((c - c_lo) * KEY_CHUNK, KEY_CHUNK)
        s = _dot(q, kt_ref[:, pl.ds(off, KEY_CHUNK)])
        s = jnp.where(mask_fn(c, off), s, NEG_INF)
        s_scr[:, pl.ds(loc, KEY_CHUNK)] = s
        m_scr[...] = jnp.maximum(m_scr[...], jnp.maximum(s[:, :LANES], s[:, LANES:]))
        return carry

    lax.fori_loop(c_lo, c_hi, scores, 0)
    m = jnp.max(m_scr[...], axis=-1, keepdims=True)
    mb = jnp.broadcast_to(m, (rows, KEY_CHUNK))
    l_scr[...] = jnp.zeros((rows, LANES), F32)
    acc_scr[...] = jnp.zeros(acc_scr.shape, F32)

    def values(c, carry):
        off = pl.multiple_of(c * KEY_CHUNK, KEY_CHUNK)
        loc = pl.multiple_of((c - c_lo) * KEY_CHUNK, KEY_CHUNK)
        e = jnp.exp(s_scr[:, pl.ds(loc, KEY_CHUNK)] - mb)
        l_scr[...] += e[:, :LANES] + e[:, LANES:]
        acc_scr[...] += _dot_nt(e.astype(BF16), vt_ref[:, pl.ds(off, KEY_CHUNK)])
        return carry

    lax.fori_loop(c_lo, c_hi, values, 0)
    l = jnp.sum(l_scr[...], axis=-1, keepdims=True)
    return jnp.where(m > 0.5 * NEG_INF, acc_scr[...] / jnp.maximum(l, 1e-30), 0.0)


def _nsa_prompt_kernel(q_ref, g_ref, kct_ref, vc_ref, kst_ref, vst_ref, kwt_ref, vwt_ref, ovt_ref, exp_ref, o_ref,
                       ks_bf, vs_bf, kw_bf, vw_bf, s_scr, m_scr, l_scr, acc_scr, *, tq):
    grp = pl.program_id(1)
    i = pl.program_id(2)

    @pl.when(i == 0)
    def _():
        ks_bf[...] = kst_ref[0].astype(BF16)
        vs_bf[...] = vst_ref[0].astype(BF16)
        kw_bf[...] = kwt_ref[0].astype(BF16)
        vw_bf[...] = vwt_ref[0].astype(BF16)

    rows = NSA_GROUP * tq
    nu = kct_ref.shape[3]
    nsel = ovt_ref.shape[0]
    qb = q_ref[...]
    q = jnp.concatenate([qb[:, HEAD_DIM * h:HEAD_DIM * (h + 1)] for h in range(NSA_GROUP)], axis=0)
    t0 = i * tq
    tpos = t0 + lax.broadcasted_iota(jnp.int32, (tq, 1), 0)
    qpos = jnp.concatenate([tpos] * NSA_GROUP, axis=0)

    s = _dot(q, kct_ref[0, 0].astype(BF16))
    cend = lax.broadcasted_iota(jnp.int32, (1, nu), 1) * CMP_STRIDE + (CMP_BLOCK - 1)
    p = _softmax_rows(s, cend <= qpos)
    o_cmp = _dot(p.astype(BF16), vc_ref[0, 0].astype(BF16))

    psum = p[0:tq]
    for h in range(1, NSA_GROUP):
        psum = psum + p[h * tq:(h + 1) * tq]
    pst = psum.T
    hi = pst.astype(BF16)
    lo = (pst - hi.astype(F32)).astype(BF16)
    imp = _dot(ovt_ref[...], hi) + _dot(ovt_ref[...], lo)
    blk = lax.broadcasted_iota(jnp.int32, (nsel, tq), 0)
    tl = t0 + lax.broadcasted_iota(jnp.int32, (nsel, tq), 1)
    cur = tl // SEL_BLOCK
    forced = (blk == 0) | (blk == cur) | (blk == cur - 1)
    score = jnp.where(forced, FORCE_SCORE, jnp.where(blk * SEL_BLOCK <= tl, imp, NEG_INF))
    cnt = jnp.zeros((nsel, tq), jnp.int32)
    for jp in range(nsel):
        other = score[jp:jp + 1, :]
        ahead = (other > score) | ((other == score) & (blk > jp))
        cnt = cnt + jnp.where(ahead, 1, 0)
    sel = jnp.where(cnt < min(N_SELECT, nsel), 1.0, 0.0).T.astype(BF16)

    def sel_mask(c, off):
        bm = _dot(sel, exp_ref[:, pl.ds(off, KEY_CHUNK)])
        kpos = off + lax.broadcasted_iota(jnp.int32, (1, KEY_CHUNK), 1)
        ok = jnp.where((bm > 0.5) & (kpos <= tpos), 1.0, 0.0)
        return jnp.concatenate([ok] * NSA_GROUP, axis=0) > 0.5

    n_chunks_q = tq // KEY_CHUNK
    o_sel = _attend_chunks(q, ks_bf, vs_bf, 0, (i + 1) * n_chunks_q, sel_mask, s_scr, m_scr, l_scr, acc_scr)

    def win_mask(c, off):
        kpos = off + lax.broadcasted_iota(jnp.int32, (1, KEY_CHUNK), 1)
        d = qpos - kpos
        return (d >= 0) & (d < WINDOW)

    c_lo = jnp.maximum(i * n_chunks_q - WINDOW // KEY_CHUNK, 0)
    o_win = _attend_chunks(q, kw_bf, vw_bf, c_lo, (i + 1) * n_chunks_q, win_mask, s_scr, m_scr, l_scr, acc_scr)

    g = g_ref[...]
    outs = []
    for h in range(NSA_GROUP):
        def gate(br):
            a = g[:, br * NSA_HEADS + h:br * NSA_HEADS + h + 1]
            b = g[:, br * NSA_HEADS + NSA_GROUP + h:br * NSA_HEADS + NSA_GROUP + h + 1]
            return jnp.where(grp == 0, a, b)

        sl = slice(h * tq, (h + 1) * tq)
        outs.append(o_cmp[sl] * gate(0) + o_sel[sl] * gate(1) + o_win[sl] * gate(2))
    o_ref[...] = jnp.concatenate(outs, axis=1).astype(BF16)


def _nsa_prompt(qn, gn, kct, vc, kvs_t, kvw_t, ovt, expand, tq):
    nb, _, t = kvs_t.shape
    nq = t // tq
    nu = kct.shape[3]
    rows = NSA_GROUP * tq
    kern = functools.partial(_nsa_prompt_kernel, tq=tq)

    def kmap(b, g, i):
        return (b, g, 0)

    def vmap_(b, g, i):
        return (b, NSA_KV_HEADS + g, 0)

    return pl.pallas_call(
        kern, grid=(nb, NSA_KV_HEADS, nq),
        in_specs=[pl.BlockSpec((tq, NSA_GROUP * HEAD_DIM), lambda b, g, i: (b * nq + i, g)),
                  pl.BlockSpec((tq, LANES), lambda b, g, i: (b * nq + i, 0)),
                  pl.BlockSpec((1, 1, HEAD_DIM, nu), lambda b, g, i: (b, g, 0, 0)),
                  pl.BlockSpec((1, 1, nu, HEAD_DIM), lambda b, g, i: (b, g, 0, 0)),
                  pl.BlockSpec((1, HEAD_DIM, t), kmap), pl.BlockSpec((1, HEAD_DIM, t), vmap_),
                  pl.BlockSpec((1, HEAD_DIM, t), kmap), pl.BlockSpec((1, HEAD_DIM, t), vmap_),
                  pl.BlockSpec(ovt.shape, lambda b, g, i: (0, 0)),
                  pl.BlockSpec(expand.shape, lambda b, g, i: (0, 0))],
        out_specs=pl.BlockSpec((tq, NSA_GROUP * HEAD_DIM), lambda b, g, i: (b * nq + i, g)),
        out_shape=jax.ShapeDtypeStruct((nb * t, NSA_WIDTH), BF16),
        scratch_shapes=[pltpu.VMEM((HEAD_DIM, t), BF16)] * 4
        + [pltpu.VMEM((rows, t), F32), pltpu.VMEM((rows, LANES), F32), pltpu.VMEM((rows, LANES), F32),
           pltpu.VMEM((rows, HEAD_DIM), F32)],
        compiler_params=_cparams(("parallel", "arbitrary", "arbitrary")), name="nsa_prompt",
    )(qn, gn, kct, vc, kvs_t, kvs_t, kvw_t, kvw_t, ovt, expand)


def _diff_lambda(lam_ref, lam_init):
    lq1, lk1, lq2, lk2 = lam_ref[0:1], lam_ref[1:2], lam_ref[2:3], lam_ref[3:4]
    return (jnp.exp(jnp.sum(lq1 * lk1, axis=-1, keepdims=True)) - jnp.exp(jnp.sum(lq2 * lk2, axis=-1, keepdims=True))
            + lam_init)


def _diff_prompt_kernel(q_ref, kt_ref, vt_ref, lam_ref, o_ref, k_bf, v_bf, s_scr, m_scr, l_scr, acc_scr, *, tq,
                        lam_init):
    i = pl.program_id(2)

    @pl.when(i == 0)
    def _():
        k_bf[...] = kt_ref[0].astype(BF16)
        v_bf[...] = vt_ref[0].astype(BF16)

    lam = _diff_lambda(lam_ref, lam_init)
    qb = q_ref[...]
    tpos = i * tq + lax.broadcasted_iota(jnp.int32, (tq, 1), 0)

    def causal(c, off):
        return off + lax.broadcasted_iota(jnp.int32, (1, KEY_CHUNK), 1) <= tpos

    n_hi = (i + 1) * (tq // KEY_CHUNK)
    outs = []
    for h in range(2):
        o = []
        for c in range(2):
            lo = h * 2 * DIFF_QK_DIM + c * DIFF_QK_DIM
            q = qb[:, lo:lo + DIFF_QK_DIM]
            o.append(_attend_chunks(q, k_bf.at[lo:lo + DIFF_QK_DIM], v_bf.at[h * DIFF_V_DIM:(h + 1) * DIFF_V_DIM],
                                    0, n_hi, causal, s_scr, m_scr, l_scr, acc_scr))
        outs.append(o[0] - lam * o[1])
    o_ref[...] = jnp.concatenate(outs, axis=1)


def _diff_prompt(qd, kd_t, vd_t, lam_vec, lam_init, tq):
    nb, _, t = kd_t.shape
    nq = t // tq
    kern = functools.partial(_diff_prompt_kernel, tq=tq, lam_init=lam_init)
    return pl.pallas_call(
        kern, grid=(nb, DIFF_HEADS // 2, nq),
        in_specs=[pl.BlockSpec((tq, LANES), lambda b, hp, i: (b * nq + i, hp)),
                  pl.BlockSpec((1, LANES, t), lambda b, hp, i: (b, hp, 0)),
                  pl.BlockSpec((1, LANES, t), lambda b, hp, i: (b, hp, 0)),
                  pl.BlockSpec(lam_vec.shape, lambda b, hp, i: (0, 0))],
        out_specs=pl.BlockSpec((tq, LANES), lambda b, hp, i: (b * nq + i, hp)),
        out_shape=jax.ShapeDtypeStruct((nb * t, DIFF_WIDTH), F32),
        scratch_shapes=[pltpu.VMEM((LANES, t), BF16)] * 2
        + [pltpu.VMEM((tq, t), F32), pltpu.VMEM((tq, LANES), F32), pltpu.VMEM((tq, LANES), F32),
           pltpu.VMEM((tq, DIFF_V_DIM), F32)],
        compiler_params=_cparams(("parallel", "arbitrary", "arbitrary")), name="diff_prompt",
    )(qd, kd_t, vd_t, lam_vec)


def _mix_out(x, o_nsa, zn, od, zd, g64_ref, dow_ref, wo_ref, lam_init):
    o1 = (o_nsa * zn.astype(F32)).astype(BF16)
    odn = od * lax.rsqrt(_dot(od * od, g64_ref[...]) + NORM_EPS) * dow_ref[...] * (1.0 - lam_init)
    o2 = (odn * zd.astype(F32)).astype(BF16)
    return x + _dot(o1, wo_ref[0:NSA_WIDTH]) + _dot(o2, wo_ref[NSA_WIDTH:NSA_WIDTH + DIFF_WIDTH])


def _out_kernel(x_ref, on_ref, zn_ref, od_ref, zd_ref, g64_ref, dow_ref, wo_ref, y_ref, *, lam_init):
    y_ref[...] = _mix_out(x_ref[...], on_ref[...].astype(F32), zn_ref[...], od_ref[...], zd_ref[...], g64_ref,
                          dow_ref, wo_ref, lam_init)


def _out_proj(x2, o_nsa, zn, od, zd, prm, lam_init, tm):
    n, d = x2.shape

    def row(i):
        return (i, 0)

    def full(a):
        return pl.BlockSpec(a.shape, lambda i: (0, 0))

    consts = (prm['g64'], prm['dow'], prm['w_out'])
    return pl.pallas_call(
        functools.partial(_out_kernel, lam_init=lam_init), grid=(n // tm,),
        in_specs=[pl.BlockSpec((tm, d), row)] + [pl.BlockSpec((tm, 512), row)] * 4 + [full(a) for a in consts],
        out_specs=pl.BlockSpec((tm, d), row), out_shape=jax.ShapeDtypeStruct((n, d), F32),
        compiler_params=_cparams(("parallel",)), name="out_proj",
    )(x2, o_nsa, zn, od, zd, *consts)


def _sample_cmp_kernel(pt_ref, q_ref, *rest, pps, past, n_sel):
    pages = rest[:pps]
    (w1_ref, posl_ref, w2_ref, g64_ref, kcw_ref, cc_ref, sac_ref, sbc_ref, ov_ref, ocmp_ref, idx_ref,
     xs_ref) = rest[pps:]
    s = pl.program_id(1)
    for k in range(pps):
        row0 = pl.multiple_of((s * pps + k) * LANES, LANES)
        for c in range(2):
            xs_ref[c, pl.ds(row0, LANES), :] = pages[k][0, c * 128:(c + 1) * 128, :].T

    @pl.when(s == pl.num_programs(1) - 1)
    def _():
        nu = past // CMP_STRIDE
        nuc = min(nu, 256)
        outs = []
        for c in range(2):
            pq = jnp.concatenate([_compress_units(xs_ref, nu, u0, nuc, w1_ref, posl_ref, w2_ref, c)
                                  for u0 in range(0, nu, nuc)], axis=0)
            outs.append(_compress_finish(pq, posl_ref, w1_ref, w2_ref, c))
        kct = _kc_norm_rope(outs[0], g64_ref, kcw_ref, cc_ref, sac_ref, sbc_ref).T.astype(BF16)
        vc = outs[1].astype(BF16)
        q = q_ref[0]
        row = lax.broadcasted_iota(jnp.int32, (NSA_HEADS, 1), 0)
        first = row < NSA_GROUP
        sc = jnp.where(first, _dot(q, kct[0:HEAD_DIM]), _dot(q, kct[HEAD_DIM:2 * HEAD_DIM]))
        cend = lax.broadcasted_iota(jnp.int32, (1, nu), 1) * CMP_STRIDE + (CMP_BLOCK - 1)
        p = _softmax_rows(sc, cend <= past)
        pb = p.astype(BF16)
        ocmp_ref[0] = jnp.where(first, _dot(pb, vc[:, 0:HEAD_DIM]), _dot(pb, vc[:, HEAD_DIM:2 * HEAD_DIM]))
        psum = jnp.concatenate([jnp.sum(p[0:NSA_GROUP], axis=0, keepdims=True),
                                jnp.sum(p[NSA_GROUP:], axis=0, keepdims=True),
                                jnp.zeros((NSA_HEADS - 2, nu), F32)], axis=0)
        hi = psum.astype(BF16)
        lo = (psum - hi.astype(F32)).astype(BF16)
        imp = _dot(hi, ov_ref[...]) + _dot(lo, ov_ref[...])
        nselp = ov_ref.shape[1]
        blk = lax.broadcasted_iota(jnp.int32, (NSA_HEADS, nselp), 1)
        cur = past // SEL_BLOCK
        forced = (blk == 0) | (blk == cur) | (blk == cur - 1)
        score = jnp.where(forced, FORCE_SCORE, jnp.where(blk * SEL_BLOCK <= past, imp, NEG_INF))
        score = jnp.where(blk < n_sel, score, EXCLUDED)
        blkf = blk.astype(F32)
        lane = lax.broadcasted_iota(jnp.int32, (NSA_HEADS, LANES), 1)
        picked = jnp.zeros((NSA_HEADS, LANES), F32)
        for k in range(min(N_SELECT, n_sel)):
            mx = jnp.max(score, axis=-1, keepdims=True)
            ik = jnp.min(jnp.where(score == mx, blkf, 1e9), axis=-1, keepdims=True)
            picked = jnp.where(lane == k, ik, picked)
            score = jnp.where(blkf == ik, EXCLUDED, score)
        idx_ref[0] = picked.astype(jnp.int32)


def _sample_cmp(page_table, q8, cache_ct, prm, cend_tables, ov, past, n_sel, pps):
    db, n_pages = page_table.shape
    n_steps = n_pages // pps
    consts = (prm['cmp_w1'], prm['cmp_posl'], prm['cmp_w2'], prm['g64_128'], prm['kcw']) + tuple(cend_tables) + (ov,)

    def full(a):
        nd = a.ndim
        return pl.BlockSpec(a.shape, lambda b, s, pt: (0,) * nd)

    def page_spec(k):
        return pl.BlockSpec((1, 256, LANES), lambda b, s, pt: (pt[b, s * pps + k], 0, 0))

    kern = functools.partial(_sample_cmp_kernel, pps=pps, past=past, n_sel=n_sel)
    return pl.pallas_call(
        kern,
        grid_spec=pltpu.PrefetchScalarGridSpec(
            num_scalar_prefetch=1, grid=(db, n_steps),
            in_specs=[pl.BlockSpec((1, NSA_HEADS, HEAD_DIM), lambda b, s, pt: (b, 0, 0))]
            + [page_spec(k) for k in range(pps)] + [full(a) for a in consts],
            out_specs=[pl.BlockSpec((1, NSA_HEADS, HEAD_DIM), lambda b, s, pt: (b, 0, 0)),
                       pl.BlockSpec((1, NSA_HEADS, LANES), lambda b, s, pt: (b, 0, 0))],
            scratch_shapes=[pltpu.VMEM((2, past, LANES), F32)]),
        out_shape=[jax.ShapeDtypeStruct((db, NSA_HEADS, HEAD_DIM), F32),
                   jax.ShapeDtypeStruct((db, NSA_HEADS, LANES), jnp.int32)],
        compiler_params=_cparams(("parallel", "arbitrary")), name="sample_cmp",
    )(page_table, q8, *([cache_ct] * pps), *consts)


def _sample_attn_kernel(pt_ref, idx_ref, q_ref, knew_ref, vnew_ref, wnew_ref, win_ref, *rest, n_k, past, n_pages):
    pages = rest[:n_k]
    osel_ref, owin_ref, wout_ref = rest[n_k:]
    b = pl.program_id(0)
    g = pl.program_id(1)
    q = q_ref[0, 0]
    lane = lax.broadcasted_iota(jnp.int32, (1, LANES), 1)
    n_cached = 2 * n_pages

    ss, vts = [], []
    for k in range(n_k):
        j = idx_ref[b, g, k]
        jc = jnp.minimum(j, n_cached - 1)
        kpos = (jc // 2) * LANES + lane
        ok = (lane // SEL_BLOCK == jc % 2) & (kpos <= past) & (j < n_cached)
        ss.append(jnp.where(ok, _dot(q, pages[k][0, 0, 0].astype(BF16)), NEG_INF))
        vts.append(pages[k][0, 1, 0].astype(BF16))
    s_new = jnp.sum(q.astype(F32) * knew_ref[0, 0][0:1], axis=-1, keepdims=True)
    m = s_new
    for s in ss:
        m = jnp.maximum(m, jnp.max(s, axis=-1, keepdims=True))
    e_new = jnp.exp(s_new - m)
    l = e_new
    acc = e_new * vnew_ref[0, 0][0:1]
    for s, vt in zip(ss, vts):
        e = jnp.exp(s - m)
        l = l + jnp.sum(e, axis=-1, keepdims=True)
        acc = acc + _dot_nt(e.astype(BF16), vt)
    osel_ref[0, 0] = acc / jnp.maximum(l, 1e-30)

    w = win_ref.shape[4]
    wl = lax.broadcasted_iota(jnp.int32, (1, w), 1)
    kt = jnp.where(wl == w - 1, wnew_ref[0, 0, 0], pltpu.roll(win_ref[0, 0, 0], w - 1, 1))
    vt = jnp.where(wl == w - 1, wnew_ref[0, 1, 0], pltpu.roll(win_ref[0, 1, 0], w - 1, 1))
    wout_ref[0, 0, 0] = kt
    wout_ref[0, 1, 0] = vt
    kpos = past - (w - 1) + wl
    d = past - kpos
    p = _softmax_rows(_dot(q, kt.astype(BF16)), (d >= 0) & (d < WINDOW) & (kpos >= 0))
    owin_ref[0, 0] = _dot_nt(p.astype(BF16), vt.astype(BF16))


def _sample_attn(page_table, idx, q4, knew, vnew, wnew, win_t, cache_st, past):
    db, n_pages = page_table.shape
    n_k = idx.shape[2]
    w = win_t.shape[4]

    def per_bg(shape):
        nd = len(shape)
        return pl.BlockSpec((1, 1) + shape, lambda b, g, pt, ix: (b, g) + (0,) * nd)

    def kv_bg(last):
        return pl.BlockSpec((1, 2, 1, HEAD_DIM, last), lambda b, g, pt, ix: (b, 0, g, 0, 0))

    def page_spec(k):
        def imap(b, g, pt, ix):
            j = jnp.minimum(ix[b, g, k], 2 * n_pages - 1)
            return (pt[b, j // 2], 0, g, 0, 0)
        return pl.BlockSpec((1, 2, 1, HEAD_DIM, LANES), imap)

    kern = functools.partial(_sample_attn_kernel, n_k=n_k, past=past, n_pages=n_pages)
    o_shape = jax.ShapeDtypeStruct((db, NSA_KV_HEADS, 8, HEAD_DIM), F32)
    return pl.pallas_call(
        kern,
        grid_spec=pltpu.PrefetchScalarGridSpec(
            num_scalar_prefetch=2, grid=(db, NSA_KV_HEADS),
            in_specs=[per_bg((8, HEAD_DIM))] * 3 + [kv_bg(1), kv_bg(w)] + [page_spec(k) for k in range(n_k)],
            out_specs=[per_bg((8, HEAD_DIM)), per_bg((8, HEAD_DIM)), kv_bg(w)]),
        out_shape=[o_shape, o_shape, jax.ShapeDtypeStruct(win_t.shape, F32)],
        compiler_params=_cparams(("parallel", "arbitrary")), name="sample_attn",
    )(page_table, idx, q4, knew, vnew, wnew, win_t, *([cache_st] * n_k))


def _sample_diff_kernel(pt_ref, qbd_ref, q16_ref, k16_ref, vnew_ref, lam_ref, *rest, pps, lam_init):
    kpages = rest[:pps]
    vpages = rest[pps:2 * pps]
    o_ref, m_scr, l_scr, acc_scr = rest[2 * pps:]
    s = pl.program_id(1)
    rows = 2 * DIFF_HEADS

    @pl.when(s == 0)
    def _():
        m_scr[...] = jnp.full(m_scr.shape, NEG_INF, F32)
        l_scr[...] = jnp.zeros(l_scr.shape, F32)
        acc_scr[...] = jnp.zeros(acc_scr.shape, F32)

    qbd = qbd_ref[0]
    ss = [_dot(qbd, kp[0].astype(BF16)) for kp in kpages]
    cm = ss[0]
    for sk in ss[1:]:
        cm = jnp.maximum(cm, sk)
    m_old = m_scr[...]
    m_new = jnp.maximum(m_old, jnp.max(cm, axis=-1, keepdims=True))
    alpha = jnp.exp(m_old - m_new)
    lsum = jnp.zeros((rows, LANES), F32)
    accs = jnp.zeros(acc_scr.shape, F32)
    for sk, vp in zip(ss, vpages):
        e = jnp.exp(sk - m_new)
        lsum = lsum + e
        accs = accs + _dot_nt(e.astype(BF16), vp[0].astype(BF16))
    m_scr[...] = m_new
    l_scr[...] = alpha * l_scr[...] + jnp.sum(lsum, axis=-1, keepdims=True)
    acc_scr[...] = alpha[:, 0:1] * acc_scr[...] + accs

    @pl.when(s == pl.num_programs(1) - 1)
    def _():
        s_new = jnp.sum(q16_ref[0] * k16_ref[0], axis=-1, keepdims=True)
        m1 = m_scr[...][:, 0:1]
        m2 = jnp.maximum(m1, s_new)
        a2 = jnp.exp(m1 - m2)
        e_new = jnp.exp(s_new - m2)
        l = a2 * l_scr[...][:, 0:1] + e_new
        acc = acc_scr[...]
        col = lax.broadcasted_iota(jnp.int32, acc.shape, 1) // DIFF_V_DIM
        rowh = lax.broadcasted_iota(jnp.int32, acc.shape, 0) % DIFF_HEADS
        acc = jnp.where(col == rowh, acc, 0.0)
        o16 = acc[:, 0:DIFF_V_DIM]
        for h in range(1, DIFF_HEADS):
            o16 = o16 + acc[:, h * DIFF_V_DIM:(h + 1) * DIFF_V_DIM]
        vnew = vnew_ref[0]
        o16 = (a2 * o16 + e_new * jnp.concatenate([vnew, vnew], axis=0)) / jnp.maximum(l, 1e-30)
        o_ref[0] = o16[0:DIFF_HEADS] - _diff_lambda(lam_ref, lam_init) * o16[DIFF_HEADS:]


def _sample_diff(page_table, qbd, q16, k16, vnew, lam_vec, cache_dkt, cache_dvt, lam_init, pps):
    db, n_pages = page_table.shape
    n_steps = n_pages // pps
    rows = 2 * DIFF_HEADS

    def per_b(shape):
        nd = len(shape)
        return pl.BlockSpec((1,) + shape, lambda b, s, pt: (b,) + (0,) * nd)

    def page_spec(k):
        return pl.BlockSpec((1, 512, LANES), lambda b, s, pt: (pt[b, s * pps + k], 0, 0))

    kern = functools.partial(_sample_diff_kernel, pps=pps, lam_init=lam_init)
    return pl.pallas_call(
        kern,
        grid_spec=pltpu.PrefetchScalarGridSpec(
            num_scalar_prefetch=1, grid=(db, n_steps),
            in_specs=[per_b((rows, 512)), per_b((rows, DIFF_QK_DIM)), per_b((rows, DIFF_QK_DIM)),
                      per_b((DIFF_HEADS, DIFF_V_DIM)), pl.BlockSpec(lam_vec.shape, lambda b, s, pt: (0, 0))]
            + [page_spec(k) for k in range(pps)] * 2,
            out_specs=per_b((DIFF_HEADS, DIFF_V_DIM)),
            scratch_shapes=[pltpu.VMEM((rows, LANES), F32), pltpu.VMEM((rows, LANES), F32),
                            pltpu.VMEM((rows, 512), F32)]),
        out_shape=jax.ShapeDtypeStruct((db, DIFF_HEADS, DIFF_V_DIM), F32),
        compiler_params=_cparams(("parallel", "arbitrary")), name="sample_diff",
    )(page_table, qbd, q16, k16, vnew, lam_vec, *([cache_dkt] * pps), *([cache_dvt] * pps))


def _sample_out_kernel(x_ref, oc_ref, os_ref, ow_ref, g0_ref, g1_ref, g2_ref, zn_ref, od_ref, zd_ref, g64_ref, dow_ref,
                       wo_ref, y_ref, *, lam_init):
    o_nsa = oc_ref[...] * g0_ref[...] + os_ref[...] * g1_ref[...] + ow_ref[...] * g2_ref[...]
    y_ref[...] = _mix_out(x_ref[...], o_nsa, zn_ref[...], od_ref[...], zd_ref[...], g64_ref, dow_ref, wo_ref, lam_init)


def _sample_out(x2, branches, gates, zn, od, zd, prm, lam_init):
    args = (x2,) + tuple(branches) + tuple(gates) + (zn, od, zd, prm['g64'], prm['dow'], prm['w_out'])
    return pl.pallas_call(
        functools.partial(_sample_out_kernel, lam_init=lam_init),
        out_shape=jax.ShapeDtypeStruct(x2.shape, F32), compiler_params=_cparams(None), name="sample_out",
    )(*args)


def _layer_params(lp):
    w = lp['w_in']
    col = lambda k: w[:, _OFFS[k]:_OFFS[k + 1]]
    d = w.shape[0]
    w_std = jnp.concatenate([col(0), col(6), col(5), col(9), col(4), jnp.zeros((d, LANES - 3 * NSA_HEADS), F32)], axis=1)
    w_trn = jnp.concatenate([col(1), col(2), col(3), col(7), col(8)], axis=1).T

    def cmp_w1(w1):
        w1 = w1.reshape(2, CMP_STRIDE, HEAD_DIM, HEAD_DIM)
        eye = jnp.eye(NSA_KV_HEADS, dtype=F32)
        full = jnp.einsum('hlde,gk->lgdhke', w1, eye)
        return full.reshape(CMP_STRIDE * NSA_KV_HEADS * HEAD_DIM, 2 * NSA_KV_HEADS * HEAD_DIM)

    def cmp_pos(pos):
        pos = pos.reshape(2, CMP_STRIDE, 1, HEAD_DIM)
        rows = jnp.broadcast_to(pos, (2, CMP_STRIDE, NSA_KV_HEADS, HEAD_DIM)).reshape(2, -1)
        return jnp.concatenate([rows, jnp.zeros((6, rows.shape[1]), F32)], axis=0)

    def cmp_w2(w2):
        return jnp.kron(jnp.eye(NSA_KV_HEADS, dtype=F32), w2)

    lam_vec = jnp.concatenate([lp['lambda_q1'][None], lp['lambda_k1'][None], lp['lambda_q2'][None],
                               lp['lambda_k2'][None], jnp.zeros((4, DIFF_QK_DIM), F32)], axis=0)
    return {
        'norm_w': lp['norm_w'][None], 'w_std': w_std.astype(BF16), 'w_trn': w_trn.astype(BF16),
        'g64': _group_mean_matrix(512, HEAD_DIM), 'g32': _group_mean_matrix(512, DIFF_QK_DIM),
        'g64_128': _group_mean_matrix(LANES, HEAD_DIM),
        'qnw': jnp.tile(lp['nsa_q_norm'], NSA_HEADS)[None], 'qdw': jnp.tile(lp['diff_q_norm'], 2 * DIFF_HEADS)[None],
        'ksw': lp['nsa_ks_norm'][:, None], 'kww': lp['nsa_kw_norm'][:, None], 'kdw': lp['diff_k_norm'][:, None],
        'kcw': jnp.tile(lp['nsa_kc_norm'], NSA_KV_HEADS)[None],
        'cmp_w1': jnp.stack([cmp_w1(lp['cmp_w1_k']), cmp_w1(lp['cmp_w1_v'])]).astype(BF16),
        'cmp_posl': jnp.stack([cmp_pos(lp['cmp_pos_k']), cmp_pos(lp['cmp_pos_v'])]).astype(BF16),
        'cmp_w2': jnp.stack([cmp_w2(lp['cmp_w2_k']), cmp_w2(lp['cmp_w2_v'])]).astype(BF16),
        'dow': jnp.tile(lp['diff_out_norm'], DIFF_HEADS)[None], 'w_out': lp['w_out'].astype(BF16),
        'lam_vec': lam_vec,
    }


def _overlap_t(n_sel_pad, nu, n_cmp, n_sel):
    cs = np.arange(nu)[None, :] * CMP_STRIDE
    ss = np.arange(n_sel_pad)[:, None] * SEL_BLOCK
    ov = np.clip(np.minimum(cs + CMP_BLOCK, ss + SEL_BLOCK) - np.maximum(cs, ss), 0, None) / CMP_STRIDE
    ov = ov * (np.arange(nu)[None, :] < n_cmp) * (np.arange(n_sel_pad)[:, None] < n_sel)
    return jnp.asarray(ov, dtype=BF16)


def _prompt_layer(x, prm, lam_init):
    b, t, d = x.shape
    x2 = x.reshape(b * t, d)
    tables = _rope_tables(jnp.arange(t))
    qn, qd, zn, zd, gn, kvc_t, kvs_t, kvw_t, kd_t, vd_t = _project(x2, tables, t, prm, 512)
    nu = t // CMP_STRIDE
    n_cmp = nu - CMP_BLOCK // CMP_STRIDE + 1
    cend_tables = _rope_tables(jnp.arange(nu) * CMP_STRIDE + CMP_BLOCK - 1)[0]
    kct, vc = _compress_prompt(kvc_t, prm, cend_tables)
    n_sel = -(-t // SEL_BLOCK)
    ovt = _overlap_t(n_sel, nu, n_cmp, n_sel)
    expand = jnp.asarray(np.arange(n_sel)[:, None] == np.arange(t)[None, :] // SEL_BLOCK, dtype=BF16)
    o_nsa = _nsa_prompt(qn, gn, kct, vc, kvs_t, kvw_t, ovt, expand, 256)
    o_d = _diff_prompt(qd, kd_t, vd_t, prm['lam_vec'], lam_init, 512)
    y = _out_proj(x2, o_nsa, zn, o_d, zd, prm, lam_init, 512).reshape(b, t, d)

    def state(a_t, shape_tail):
        return jnp.transpose(a_t.reshape((b,) + shape_tail + (a_t.shape[-1],)),
                             (0, len(shape_tail) + 1) + tuple(range(1, len(shape_tail) + 1)))

    kv_tail = (2, NSA_KV_HEADS, HEAD_DIM)
    w = min(WINDOW, t)
    return y, (state(kvc_t, kv_tail), state(kvs_t, kv_tail), state(kvw_t[:, :, t - w:], kv_tail),
               state(kd_t, (DIFF_HEADS, 2, DIFF_QK_DIM)), state(vd_t, (DIFF_HEADS, DIFF_V_DIM)))


def _sample_layer(x, cache_c, cache_s, cache_dk, cache_dv, win, page_table, prm, lam_init):
    db, t, d = x.shape
    assert t == 1 and db % 8 == 0
    n_pool, page = cache_c.shape[:2]
    assert page == LANES
    n_pages = page_table.shape[1]
    past = n_pages * page
    x2 = x.reshape(db, d)
    tables = _rope_tables(jnp.full((db,), past))
    qn, qd, zn, zd, gn, kvc_t, kvs_t, kvw_t, kd_t, vd_t = _project(x2, tables, db, prm, db)

    cache_ct = jnp.transpose(cache_c, (0, 2, 3, 4, 1)).reshape(n_pool, 256, page)
    cache_st = jnp.transpose(cache_s, (0, 2, 3, 4, 1))
    cache_dkt = jnp.transpose(cache_dk, (0, 2, 3, 4, 1)).reshape(n_pool, 512, page)
    cache_dvt = jnp.transpose(cache_dv, (0, 2, 3, 1)).reshape(n_pool, 512, page)
    win_t = jnp.transpose(win, (0, 2, 3, 4, 1))

    nu = past // CMP_STRIDE
    n_cmp = nu - CMP_BLOCK // CMP_STRIDE + 1
    n_sel = -(-(past + t) // SEL_BLOCK)
    n_sel_pad = -(-n_sel // LANES) * LANES
    ov = _overlap_t(n_sel_pad, nu, n_cmp, n_sel).T
    cend_tables = _rope_tables(jnp.arange(nu) * CMP_STRIDE + CMP_BLOCK - 1)[0]
    pps = min(16, n_pages)
    o_cmp, picked = _sample_cmp(page_table, qn.reshape(db, NSA_HEADS, HEAD_DIM), cache_ct, prm, cend_tables, ov, past,
                                n_sel, pps)
    n_k = min(N_SELECT, n_sel)
    idx = picked[:, :NSA_KV_HEADS, :n_k]

    def rows8(a):
        return jnp.pad(a, ((0, 0), (0, 0), (0, 8 - a.shape[2]), (0, 0)))

    kvs = kvs_t[0].T.reshape(db, 2, NSA_KV_HEADS, 1, HEAD_DIM)
    kvw = kvw_t[0].T.reshape(db, 2, NSA_KV_HEADS, HEAD_DIM, 1)
    q4 = rows8(qn.reshape(db, NSA_KV_HEADS, NSA_GROUP, HEAD_DIM))
    o_sel, o_win, win_new = _sample_attn(page_table, idx, q4, rows8(kvs[:, 0]), rows8(kvs[:, 1]), kvw, win_t,
                                         cache_st, past)

    qd3 = qd.astype(F32).reshape(db, DIFF_HEADS, 2, DIFF_QK_DIM)
    q16 = jnp.transpose(qd3, (0, 2, 1, 3)).reshape(db, 2 * DIFF_HEADS, DIFF_QK_DIM)
    own = np.zeros((2 * DIFF_HEADS, DIFF_QK_COLS), np.float32)
    for c in range(2):
        for h in range(DIFF_HEADS):
            lo = h * 2 * DIFF_QK_DIM + c * DIFF_QK_DIM
            own[c * DIFF_HEADS + h, lo:lo + DIFF_QK_DIM] = 1.0
    qbd = (qd.astype(F32)[:, None, :] * own[None]).astype(BF16)
    kd3 = kd_t[0].T.reshape(db, DIFF_HEADS, 2, DIFF_QK_DIM)
    k16 = jnp.transpose(kd3, (0, 2, 1, 3)).reshape(db, 2 * DIFF_HEADS, DIFF_QK_DIM)
    vnew = vd_t[0].T.reshape(db, DIFF_HEADS, DIFF_V_DIM)
    o_d = _sample_diff(page_table, qbd, q16, k16, vnew, prm['lam_vec'], cache_dkt, cache_dvt, lam_init,
                       min(16, n_pages))

    gates = [jnp.repeat(gn[:, br * NSA_HEADS:(br + 1) * NSA_HEADS], HEAD_DIM, axis=1) for br in range(3)]
    branches = (o_cmp.reshape(db, NSA_WIDTH), o_sel[:, :, :NSA_GROUP].reshape(db, NSA_WIDTH),
                o_win[:, :, :NSA_GROUP].reshape(db, NSA_WIDTH))
    y = _sample_out(x2, branches, gates, zn, o_d.reshape(db, DIFF_WIDTH), zd, prm, lam_init).reshape(db, t, d)

    kv_shape = (db, t, 2, NSA_KV_HEADS, HEAD_DIM)
    states = (kvc_t[0].T.reshape(kv_shape), kvs_t[0].T.reshape(kv_shape), jnp.transpose(win_new, (0, 4, 1, 2, 3)),
              kd3.reshape(db, t, DIFF_HEADS, 2, DIFF_QK_DIM), vnew.reshape(db, t, DIFF_HEADS, DIFF_V_DIM))
    return y, states


def kernel(x_prompt, x_sample, cache_nsa_cmp_kv, cache_nsa_sel_kv, cache_diff_k, cache_diff_v, state_nsa_win_kv, page_table, norm_w, w_in, nsa_q_norm, nsa_kc_norm, nsa_ks_norm, nsa_kw_norm, cmp_pos_k, cmp_w1_k, cmp_w2_k, cmp_pos_v, cmp_w1_v, cmp_w2_v, diff_q_norm, diff_k_norm, lambda_q1, lambda_k1, lambda_q2, lambda_k2, diff_out_norm, w_out):
    depth = w_in.shape[0]
    y_p, y_s = x_prompt, x_sample
    p_states, s_states = [], []
    for layer in range(depth):
        lp = {
            'norm_w': norm_w[layer], 'w_in': w_in[layer],
            'nsa_q_norm': nsa_q_norm[layer], 'nsa_kc_norm': nsa_kc_norm[layer],
            'nsa_ks_norm': nsa_ks_norm[layer], 'nsa_kw_norm': nsa_kw_norm[layer],
            'cmp_pos_k': cmp_pos_k[layer], 'cmp_w1_k': cmp_w1_k[layer], 'cmp_w2_k': cmp_w2_k[layer],
            'cmp_pos_v': cmp_pos_v[layer], 'cmp_w1_v': cmp_w1_v[layer], 'cmp_w2_v': cmp_w2_v[layer],
            'diff_q_norm': diff_q_norm[layer], 'diff_k_norm': diff_k_norm[layer],
            'lambda_q1': lambda_q1[layer], 'lambda_k1': lambda_k1[layer],
            'lambda_q2': lambda_q2[layer], 'lambda_k2': lambda_k2[layer],
            'diff_out_norm': diff_out_norm[layer], 'w_out': w_out[layer],
        }
        lam_init = 0.8 - 0.6 * math.exp(-0.3 * layer)
        prm = _layer_params(lp)
        y_p, ps = _prompt_layer(y_p, prm, lam_init)
        y_s, ss = _sample_layer(y_s, cache_nsa_cmp_kv[layer], cache_nsa_sel_kv[layer], cache_diff_k[layer],
                                cache_diff_v[layer], state_nsa_win_kv[layer], page_table, prm, lam_init)
        p_states.append(ps)
        s_states.append(ss)
    p_c, p_s, p_w, p_dk, p_dv = [jnp.stack(t, axis=0) for t in zip(*p_states)]
    s_c, s_s, s_w, s_dk, s_dv = [jnp.stack(t, axis=0) for t in zip(*s_states)]
    return (y_p, y_s, p_c, p_s, p_w, p_dk, p_dv, s_c, s_s, s_w, s_dk, s_dv)
```

```python
import functools
import math

import jax
import jax.numpy as jnp
import numpy as np
from jax import lax
from jax.experimental import pallas as pl
from jax.experimental.pallas import tpu as pltpu

HEAD_DIM = 64
NSA_HEADS = 8
NSA_KV_HEADS = 2
NSA_GROUP = NSA_HEADS // NSA_KV_HEADS
NSA_WIDTH = NSA_HEADS * HEAD_DIM
CMP_BLOCK = 32
CMP_STRIDE = 16
SEL_BLOCK = 64
N_SELECT = 16
WINDOW = 512
DIFF_HEADS = 8
DIFF_QK_DIM = 32
DIFF_V_DIM = 64
DIFF_WIDTH = DIFF_HEADS * DIFF_V_DIM
KV_COLS = 2 * NSA_KV_HEADS * HEAD_DIM
DIFF_QK_COLS = DIFF_HEADS * 2 * DIFF_QK_DIM
ROPE_THETA = 500000.0
ROT_FRACTION = 4
NORM_EPS = 1e-6
NEG_INF = -1e30
FORCE_SCORE = 1e9
EXCLUDED = -3e38
LOG2E = 1.4426950408889634

LANES = 128
BF16_ROWS = 16
KEY_CHUNK = 512
VMEM_LIMIT = 56 * 1024 * 1024

F32 = jnp.float32
BF16 = jnp.bfloat16

_SIZES = (NSA_WIDTH, KV_COLS, KV_COLS, KV_COLS, 3 * NSA_HEADS, NSA_WIDTH, DIFF_QK_COLS, DIFF_QK_COLS, DIFF_WIDTH,
          DIFF_WIDTH)
_OFFS = tuple(int(v) for v in np.concatenate([[0], np.cumsum(_SIZES)]))
_ROW_ORDER = (0, 1, 2, 3, 5, 6, 7, 8, 9, 4)
GATE_ROWS = 32
_ROW_SIZES = tuple(_SIZES[k] for k in _ROW_ORDER[:-1]) + (GATE_ROWS,)
_ROW_OFFS = tuple(int(v) for v in np.concatenate([[0], np.cumsum(_ROW_SIZES)]))
PROJ_ROWS = _ROW_OFFS[-1]


def _cparams(sem):
    return pltpu.CompilerParams(dimension_semantics=sem, vmem_limit_bytes=VMEM_LIMIT)


def _sigmoid(x):
    return 1.0 / (1.0 + jnp.exp(-x))


def _dot(a, b):
    return jnp.dot(a, b, preferred_element_type=F32)


def _dot_nt(a, b):
    return lax.dot_general(a, b, (((1,), (1,)), ((), ())), preferred_element_type=F32)


def _rope_lanes(x, c, sa, sb, half):
    parts = []
    for k in range(x.shape[1] // LANES):
        xk = x[:, LANES * k:LANES * (k + 1)]
        parts.append(xk * c + pltpu.roll(xk, LANES - half, 1) * sa + pltpu.roll(xk, half, 1) * sb)
    return jnp.concatenate(parts, axis=1) if len(parts) > 1 else parts[0]


def _norm_rope_rows(x, w_ref, c, s, groups, gd):
    tm = x.shape[1]
    x3 = x.reshape(groups, gd, tm)
    r = lax.rsqrt(jnp.mean(x3 * x3, axis=1, keepdims=True) + NORM_EPS)
    x3 = (x3 * r) * w_ref[...].reshape(1, gd, 1)
    if gd == HEAD_DIM:
        x1 = x3[:, 0:8]
        x2 = x3[:, 8:16]
        parts = [x1 * c - x2 * s, x2 * c + x1 * s, x3[:, 16:]]
    else:
        rot = x3[:, 0:8].reshape(groups * 8, tm)
        first = lax.broadcasted_iota(jnp.int32, rot.shape, 0) % 8 < 4
        swapped = jnp.where(first, pltpu.roll(rot, rot.shape[0] - 4, 0), pltpu.roll(rot, 4, 0))
        parts = [rot.reshape(groups, 8, tm) * c + swapped.reshape(groups, 8, tm) * s, x3[:, 8:]]
    return jnp.concatenate(parts, axis=1).reshape(groups * gd, tm)


def _proj_kernel(x_ref, nw_ref, wt_ref, qnw_ref, ksw_ref, kww_ref, qdw_ref, kdw_ref, c8_ref, s8_ref, c4_ref, s4_ref,
                 qn_ref, kvc_ref, kvs_ref, kvw_ref, zn_ref, qd_ref, kd_ref, vd_ref, zd_ref, g_ref, *, qn_scale,
                 qd_scale):
    x = x_ref[...]
    ms = jnp.mean(x * x, axis=-1, keepdims=True)
    ht = ((x * lax.rsqrt(ms + NORM_EPS)) * nw_ref[...]).T.astype(BF16)

    def rows(k):
        return _dot(wt_ref[_ROW_OFFS[k]:_ROW_OFFS[k + 1]], ht)

    c8, s8, c4, s4 = c8_ref[...], s8_ref[...], c4_ref[...], s4_ref[...]
    qn_ref[0] = (_norm_rope_rows(rows(0), qnw_ref, c8, s8, NSA_HEADS, HEAD_DIM) * qn_scale).astype(BF16)
    kvc_ref[0] = rows(1)
    kvs = rows(2)
    kvs_ref[0, 0:128] = _norm_rope_rows(kvs[0:128], ksw_ref, c8, s8, NSA_KV_HEADS, HEAD_DIM)
    kvs_ref[0, 128:256] = kvs[128:256]
    kvw = rows(3)
    kvw_ref[0, 0:128] = _norm_rope_rows(kvw[0:128], kww_ref, c8, s8, NSA_KV_HEADS, HEAD_DIM)
    kvw_ref[0, 128:256] = kvw[128:256]
    zn = rows(4)
    zn_ref[0] = (zn * _sigmoid(zn)).astype(BF16)
    qd_ref[0] = (_norm_rope_rows(rows(5), qdw_ref, c4, s4, 2 * DIFF_HEADS, DIFF_QK_DIM) * qd_scale).astype(BF16)
    kd_ref[0] = _norm_rope_rows(rows(6), kdw_ref, c4, s4, 2 * DIFF_HEADS, DIFF_QK_DIM)
    vd_ref[0] = rows(7)
    zd = rows(8)
    zd_ref[0] = (zd * _sigmoid(zd)).astype(BF16)
    g_ref[0] = _sigmoid(rows(9))


def _rope_tables(pos):
    pos = pos.astype(F32)
    lane = np.arange(LANES)

    def tables(group, half):
        inv = ROPE_THETA ** (-jnp.arange(half, dtype=F32) / half)
        ang = pos[:, None] * inv
        cos, sin = jnp.cos(ang), jnp.sin(ang)
        m = lane % group
        cosl, sinl = cos[:, m % half], sin[:, m % half]
        c = jnp.where(m < 2 * half, cosl, 1.0)
        sa = jnp.where(m < half, -sinl, 0.0)
        sb = jnp.where((m >= half) & (m < 2 * half), sinl, 0.0)
        return (c, sa, sb), (cos.T, sin.T)

    tq, (c8, s8) = tables(HEAD_DIM, HEAD_DIM // ROT_FRACTION // 2)
    td, (c4, s4) = tables(DIFF_QK_DIM, DIFF_QK_DIM // ROT_FRACTION // 2)
    c4 = jnp.concatenate([c4, c4], axis=0)
    s4 = jnp.concatenate([-s4, s4], axis=0)
    return tq, td, (c8, s8), (c4, s4)


def _group_mean_matrix(width, group):
    idx = np.arange(width)
    return jnp.asarray((idx[:, None] // group == idx[None, :] // group) / group, dtype=F32)


def _project(x2, pos_tables, t_per_batch, prm, tm, log2e_in_q):
    n, d = x2.shape
    nb = n // t_per_batch
    nt = t_per_batch // tm
    (c8, s8), (c4, s4) = pos_tables[2], pos_tables[3]
    fold = LOG2E if log2e_in_q else 1.0

    def full(a):
        return pl.BlockSpec(a.shape, lambda i: (0, 0))

    def out(i):
        return (i // nt, 0, i % nt)

    consts = (prm['norm_w'], prm['w_t'], prm['qnw'], prm['ksw'], prm['kww'], prm['qdw'], prm['kdw'])
    dts = (BF16, F32, F32, F32, BF16, BF16, F32, F32, BF16, F32)
    kern = functools.partial(_proj_kernel, qn_scale=HEAD_DIM ** -0.5 * fold, qd_scale=DIFF_QK_DIM ** -0.5 * fold)
    return pl.pallas_call(
        kern, grid=(n // tm,),
        in_specs=[pl.BlockSpec((tm, d), lambda i: (i, 0))] + [full(a) for a in consts]
        + [pl.BlockSpec((8, tm), lambda i: (0, i % nt))] * 4,
        out_specs=[pl.BlockSpec((1, r, tm), out) for r in _ROW_SIZES],
        out_shape=[jax.ShapeDtypeStruct((nb, r, t_per_batch), dt) for r, dt in zip(_ROW_SIZES, dts)],
        compiler_params=_cparams(("parallel",)), name="proj",
    )(x2, *consts, c8, s8, c4, s4)


def _compress_units(xs_ref, nu, u0, nuc, w1_ref, posl_ref, w2_ref, c):
    lhs = jnp.concatenate(
        [xs_ref[c, pl.ds(u0 * CMP_STRIDE + l, nuc, stride=CMP_STRIDE), :] for l in range(CMP_STRIDE)],
        axis=1).astype(BF16)
    return _dot(lhs, w1_ref[c])


def _compress_finish(pq, posl_ref, w1_ref, w2_ref, c):
    nu = pq.shape[0]
    pb = _dot(posl_ref[c], w1_ref[c])
    bias = pb[0:1, 0:128] + pb[1:2, 128:256]
    hid = pq[:, 0:128] + pltpu.roll(pq[:, 128:256], nu - 1, 0) + bias
    hid = hid * _sigmoid(hid)
    return _dot(hid.astype(BF16), w2_ref[c])


def _kc_norm_rope(kc, g64_ref, kcw_ref, cc_ref, sac_ref, sbc_ref):
    kc = kc * lax.rsqrt(_dot(kc * kc, g64_ref[...]) + NORM_EPS) * kcw_ref[...]
    return _rope_lanes(kc, cc_ref[...], sac_ref[...], sbc_ref[...], 8)


def _cmp_prompt_kernel(kvc_ref, w1_ref, posl_ref, w2_ref, g64_ref, kcw_ref, cc_ref, sac_ref, sbc_ref,
                       kc_ref, vct_ref, xs_ref):
    t = kvc_ref.shape[2]
    nu = t // CMP_STRIDE
    for c in range(2):
        xs_ref[c] = kvc_ref[0, c * 128:(c + 1) * 128, :].T
    outs = []
    for c in range(2):
        pq = _compress_units(xs_ref, nu, 0, nu, w1_ref, posl_ref, w2_ref, c)
        outs.append(_compress_finish(pq, posl_ref, w1_ref, w2_ref, c))
    kc = _kc_norm_rope(outs[0], g64_ref, kcw_ref, cc_ref, sac_ref, sbc_ref)
    vct = outs[1].T
    for g in range(NSA_KV_HEADS):
        kc_ref[0, g] = kc[:, g * HEAD_DIM:(g + 1) * HEAD_DIM]
        vct_ref[0, g] = vct[g * HEAD_DIM:(g + 1) * HEAD_DIM]


def _compress_prompt(kvc_t, prm, cend_tables):
    nb, _, t = kvc_t.shape
    nu = t // CMP_STRIDE
    cc, sac, sbc = cend_tables

    def full(a):
        nd = a.ndim
        return pl.BlockSpec(a.shape, lambda b: (0,) * nd)

    consts = (prm['cmp_w1'], prm['cmp_posl'], prm['cmp_w2'], prm['g64_128'], prm['kcw'], cc, sac, sbc)
    return pl.pallas_call(
        _cmp_prompt_kernel, grid=(nb,),
        in_specs=[pl.BlockSpec((1, 256, t), lambda b: (b, 0, 0))] + [full(a) for a in consts],
        out_specs=[pl.BlockSpec((1, NSA_KV_HEADS, nu, HEAD_DIM), lambda b: (b, 0, 0, 0)),
                   pl.BlockSpec((1, NSA_KV_HEADS, HEAD_DIM, nu), lambda b: (b, 0, 0, 0))],
        out_shape=[jax.ShapeDtypeStruct((nb, NSA_KV_HEADS, nu, HEAD_DIM), F32),
                   jax.ShapeDtypeStruct((nb, NSA_KV_HEADS, HEAD_DIM, nu), F32)],
        scratch_shapes=[pltpu.VMEM((2, t, LANES), F32)],
        compiler_params=_cparams(("parallel",)), name="cmp_prompt",
    )(kvc_t, *consts)


def _softmax_rows(s, mask):
    sm = jnp.where(mask, s, NEG_INF)
    m = jnp.max(sm, axis=-1, keepdims=True)
    e = jnp.where(mask, jnp.exp(sm - m), 0.0)
    return e / jnp.maximum(jnp.sum(e, axis=-1, keepdims=True), 1e-30)


def _ones_rows(t):
    return jnp.where(lax.broadcasted_iota(jnp.int32, (BF16_ROWS, t), 0) == 0, 1.0, 0.0).astype(BF16)


def _flash_init(m_scr, acc_scr):
    m_scr[...] = jnp.full(m_scr.shape, NEG_INF, F32)
    acc_scr[...] = jnp.zeros(acc_scr.shape, F32)


def _flash_run(problems, c_lo, c_hi, m_scr, acc_scr):
    def body(c, carry):
        off = pl.multiple_of(c * KEY_CHUNK, KEY_CHUNK)
        for n, qt, k_ref, v_ref, bias_fn in problems:
            s = _dot(k_ref[pl.ds(off, KEY_CHUNK), :], qt) + bias_fn(off)
            m_old = m_scr[n, 0:1]
            m_new = jnp.maximum(m_old, jnp.max(s, axis=0, keepdims=True))
            p = jnp.exp2(s - m_new).astype(BF16)
            acc_scr[n] = jnp.exp2(m_old - m_new) * acc_scr[n] + _dot(v_ref[:, pl.ds(off, KEY_CHUNK)], p)
            m_scr[n] = jnp.broadcast_to(m_new, m_scr.shape[1:])
        return carry

    lax.fori_loop(c_lo, c_hi, body, 0)


def _flash_finish(n, m_scr, acc_scr):
    dv = acc_scr.shape[1] - BF16_ROWS
    acc = acc_scr[n]
    return jnp.where(m_scr[n, 0:1] > 0.5 * NEG_INF, acc[0:dv] / jnp.maximum(acc[dv:dv + 1], 1e-30), 0.0)


def _nsa_prompt_kernel(q_ref, g_ref, kc_ref, vct_ref, kst_ref, vst_ref, kwt_ref, vwt_ref, ovt_ref, exp_ref, o_ref,
                       ks_sd, kw_sd, vs_aug, vw_aug, m_scr, acc_scr, *, tq):
    grp = pl.program_id(1)
    i = pl.program_id(2)
    t = vst_ref.shape[2]

    @pl.when(i == 0)
    def _():
        def group_cols(kt_ref):
            kk = kt_ref[0].T
            return jnp.where(grp == 0, kk[:, 0:HEAD_DIM], kk[:, HEAD_DIM:2 * HEAD_DIM]).astype(BF16)

        ks_sd[...] = group_cols(kst_ref)
        kw_sd[...] = group_cols(kwt_ref)
        vs_aug[0:HEAD_DIM] = vst_ref[0].astype(BF16)
        vs_aug[HEAD_DIM:HEAD_DIM + BF16_ROWS] = _ones_rows(t)
        vw_aug[0:HEAD_DIM] = vwt_ref[0].astype(BF16)
        vw_aug[HEAD_DIM:HEAD_DIM + BF16_ROWS] = _ones_rows(t)

    nu = kc_ref.shape[2]
    nsel = ovt_ref.shape[0]
    qb = q_ref[0]
    qt = jnp.concatenate([qb[HEAD_DIM * h:HEAD_DIM * (h + 1)] for h in range(NSA_GROUP)], axis=1)
    t0 = i * tq
    tpos = t0 + lax.broadcasted_iota(jnp.int32, (1, tq), 1)
    qpos = jnp.concatenate([tpos] * NSA_GROUP, axis=1)

    s = _dot(kc_ref[0, 0].astype(BF16), qt)
    cend = lax.broadcasted_iota(jnp.int32, (nu, 1), 0) * CMP_STRIDE + (CMP_BLOCK - 1)
    vis = cend <= qpos
    sm = jnp.where(vis, s, NEG_INF)
    e = jnp.where(vis, jnp.exp2(sm - jnp.max(sm, axis=0, keepdims=True)), 0.0)
    p = e / jnp.maximum(jnp.sum(e, axis=0, keepdims=True), 1e-30)
    o_cmp = _dot(vct_ref[0, 0].astype(BF16), p.astype(BF16))

    psum = p[:, 0:tq]
    for h in range(1, NSA_GROUP):
        psum = psum + p[:, h * tq:(h + 1) * tq]
    hi = psum.astype(BF16)
    lo = (psum - hi.astype(F32)).astype(BF16)
    imp = _dot(ovt_ref[...], hi) + _dot(ovt_ref[...], lo)
    blk = lax.broadcasted_iota(jnp.int32, (nsel, tq), 0)
    tl = t0 + lax.broadcasted_iota(jnp.int32, (nsel, tq), 1)
    cur = tl // SEL_BLOCK
    forced = (blk == 0) | (blk == cur) | (blk == cur - 1)
    score = jnp.where(forced, FORCE_SCORE, jnp.where(blk * SEL_BLOCK <= tl, imp, NEG_INF))
    cnt = jnp.zeros((nsel, tq), jnp.int32)
    for jp in range(nsel):
        other = score[jp:jp + 1, :]
        ahead = (other > score) | ((other == score) & (blk > jp))
        cnt = cnt + jnp.where(ahead, 1, 0)
    sel_bias = jnp.where(cnt < min(N_SELECT, nsel), 0.0, NEG_INF).astype(BF16)

    def kpos_of(off):
        return off + lax.broadcasted_iota(jnp.int32, (KEY_CHUNK, 1), 0)

    def sel_mask(off):
        bias = _dot(exp_ref[pl.ds(off, KEY_CHUNK), :], sel_bias)
        bias = jnp.where(kpos_of(off) <= tpos, bias, NEG_INF)
        return jnp.concatenate([bias] * NSA_GROUP, axis=1)

    def win_mask(off):
        d = tpos - kpos_of(off)
        bias = jnp.where((d >= 0) & (d < WINDOW), 0.0, NEG_INF)
        return jnp.concatenate([bias] * NSA_GROUP, axis=1)

    n_chunks_q = tq // KEY_CHUNK
    c_hi = (i + 1) * n_chunks_q
    c_win = jnp.maximum(i * n_chunks_q - WINDOW // KEY_CHUNK, 0)
    sel_p = (0, qt, ks_sd, vs_aug, sel_mask)
    win_p = (1, qt, kw_sd, vw_aug, win_mask)
    _flash_init(m_scr, acc_scr)
    _flash_run([sel_p], 0, c_win, m_scr, acc_scr)
    _flash_run([sel_p, win_p], c_win, c_hi, m_scr, acc_scr)
    o_sel = _flash_finish(0, m_scr, acc_scr)
    o_win = _flash_finish(1, m_scr, acc_scr)

    for h in range(NSA_GROUP):
        def gate(br):
            return g_ref[0, pl.ds(br * NSA_HEADS + grp * NSA_GROUP + h, 1), :]

        sl = slice(h * tq, (h + 1) * tq)
        o_ref[0, h * HEAD_DIM:(h + 1) * HEAD_DIM, :] = (
            o_cmp[:, sl] * gate(0) + o_sel[:, sl] * gate(1) + o_win[:, sl] * gate(2)).astype(BF16)


def _nsa_prompt(qn_t, g_t, kc, vct, kvs_t, kvw_t, ovt, expand_t, tq):
    nb, _, t = kvs_t.shape
    nq = t // tq
    nu = kc.shape[2]
    r = NSA_GROUP * tq
    kern = functools.partial(_nsa_prompt_kernel, tq=tq)

    def kmap(b, g, i):
        return (b, 0, 0)

    def vmap_(b, g, i):
        return (b, NSA_KV_HEADS + g, 0)

    def qmap(b, g, i):
        return (b, g, i)

    return pl.pallas_call(
        kern, grid=(nb, NSA_KV_HEADS, nq),
        in_specs=[pl.BlockSpec((1, NSA_GROUP * HEAD_DIM, tq), qmap),
                  pl.BlockSpec((1, GATE_ROWS, tq), lambda b, g, i: (b, 0, i)),
                  pl.BlockSpec((1, 1, nu, HEAD_DIM), lambda b, g, i: (b, g, 0, 0)),
                  pl.BlockSpec((1, 1, HEAD_DIM, nu), lambda b, g, i: (b, g, 0, 0)),
                  pl.BlockSpec((1, 2 * HEAD_DIM, t), kmap), pl.BlockSpec((1, HEAD_DIM, t), vmap_),
                  pl.BlockSpec((1, 2 * HEAD_DIM, t), kmap), pl.BlockSpec((1, HEAD_DIM, t), vmap_),
                  pl.BlockSpec(ovt.shape, lambda b, g, i: (0, 0)),
                  pl.BlockSpec(expand_t.shape, lambda b, g, i: (0, 0))],
        out_specs=pl.BlockSpec((1, NSA_GROUP * HEAD_DIM, tq), qmap),
        out_shape=jax.ShapeDtypeStruct((nb, NSA_WIDTH, t), BF16),
        scratch_shapes=[pltpu.VMEM((t, HEAD_DIM), BF16)] * 2 + [pltpu.VMEM((HEAD_DIM + BF16_ROWS, t), BF16)] * 2
        + [pltpu.VMEM((2, 8, r), F32), pltpu.VMEM((2, HEAD_DIM + BF16_ROWS, r), F32)],
        compiler_params=_cparams(("parallel", "arbitrary", "arbitrary")), name="nsa_prompt",
    )(qn_t, g_t, kc, vct, kvs_t, kvs_t, kvw_t, kvw_t, ovt, expand_t)


def _diff_lambda(lam_ref, lam_init):
    lq1, lk1, lq2, lk2 = lam_ref[0:1], lam_ref[1:2], lam_ref[2:3], lam_ref[3:4]
    return (jnp.exp(jnp.sum(lq1 * lk1, axis=-1, keepdims=True)) - jnp.exp(jnp.sum(lq2 * lk2, axis=-1, keepdims=True))
            + lam_init)


def _diff_prompt_kernel(q_ref, kt_ref, vt_ref, lam_ref, o_ref, k_sd, v_aug, m_scr, acc_scr, *, tq, lam_init):
    i = pl.program_id(2)
    t = kt_ref.shape[2]

    @pl.when(i == 0)
    def _():
        kk = kt_ref[0].T
        for j in range(4):
            k_sd[j] = kk[:, j * DIFF_QK_DIM:(j + 1) * DIFF_QK_DIM].astype(BF16)
        for h in range(2):
            v_aug[h, 0:DIFF_V_DIM] = vt_ref[0, h * DIFF_V_DIM:(h + 1) * DIFF_V_DIM, :].astype(BF16)
            v_aug[h, DIFF_V_DIM:DIFF_V_DIM + BF16_ROWS] = _ones_rows(t)

    lam = _diff_lambda(lam_ref, lam_init)
    tpos = i * tq + lax.broadcasted_iota(jnp.int32, (1, tq), 1)

    def causal(off):
        kpos = off + lax.broadcasted_iota(jnp.int32, (KEY_CHUNK, 1), 0)
        return jnp.where(kpos <= tpos, 0.0, NEG_INF)

    problems = [(j, q_ref[0, j * DIFF_QK_DIM:(j + 1) * DIFF_QK_DIM, :], k_sd.at[j], v_aug.at[j // 2], causal)
                for j in range(4)]
    _flash_init(m_scr, acc_scr)
    _flash_run(problems, 0, (i + 1) * (tq // KEY_CHUNK), m_scr, acc_scr)
    for h in range(2):
        o_ref[0, h * DIFF_V_DIM:(h + 1) * DIFF_V_DIM, :] = (
            _flash_finish(2 * h, m_scr, acc_scr) - lam * _flash_finish(2 * h + 1, m_scr, acc_scr))


def _diff_prompt(qd_t, kd_t, vd_t, lam_vec, lam_init, tq):
    nb, _, t = kd_t.shape
    nq = t // tq
    kern = functools.partial(_diff_prompt_kernel, tq=tq, lam_init=lam_init)
    return pl.pallas_call(
        kern, grid=(nb, DIFF_HEADS // 2, nq),
        in_specs=[pl.BlockSpec((1, LANES, tq), lambda b, hp, i: (b, hp, i)),
                  pl.BlockSpec((1, LANES, t), lambda b, hp, i: (b, hp, 0)),
                  pl.BlockSpec((1, LANES, t), lambda b, hp, i: (b, hp, 0)),
                  pl.BlockSpec(lam_vec.shape, lambda b, hp, i: (0, 0))],
        out_specs=pl.BlockSpec((1, LANES, tq), lambda b, hp, i: (b, hp, i)),
        out_shape=jax.ShapeDtypeStruct((nb, DIFF_WIDTH, t), F32),
        scratch_shapes=[pltpu.VMEM((4, t, DIFF_QK_DIM), BF16), pltpu.VMEM((2, DIFF_V_DIM + BF16_ROWS, t), BF16),
                        pltpu.VMEM((4, 8, tq), F32), pltpu.VMEM((4, DIFF_V_DIM + BF16_ROWS, tq), F32)],
        compiler_params=_cparams(("parallel", "arbitrary", "arbitrary")), name="diff_prompt",
    )(qd_t, kd_t, vd_t, lam_vec)


def _out_kernel(x_ref, on_ref, zn_ref, od_ref, zd_ref, dow_ref, wo_ref, y_ref, *, lam_init):
    tm = x_ref.shape[0]
    o1 = on_ref[0].astype(F32) * zn_ref[0].astype(F32)
    od = od_ref[0].reshape(DIFF_HEADS, DIFF_V_DIM, tm)
    od = od * lax.rsqrt(jnp.mean(od * od, axis=1, keepdims=True) + NORM_EPS)
    od = (od * dow_ref[...].reshape(1, DIFF_V_DIM, 1) * (1.0 - lam_init)).reshape(DIFF_WIDTH, tm)
    o2 = od * zd_ref[0].astype(F32)
    o = jnp.concatenate([o1, o2], axis=0).T.astype(BF16)
    y_ref[...] = x_ref[...] + _dot(o, wo_ref[...])


def _out_proj(x2, o_nsa_t, zn_t, od_t, zd_t, prm, lam_init, tm):
    n, d = x2.shape
    t = o_nsa_t.shape[2]
    nt = t // tm

    def tok(i):
        return (i // nt, 0, i % nt)

    def full(a):
        return pl.BlockSpec(a.shape, lambda i: (0, 0))

    consts = (prm['dow_col'], prm['w_out'])
    return pl.pallas_call(
        functools.partial(_out_kernel, lam_init=lam_init), grid=(n // tm,),
        in_specs=[pl.BlockSpec((tm, d), lambda i: (i, 0))] + [pl.BlockSpec((1, 512, tm), tok)] * 4
        + [full(a) for a in consts],
        out_specs=pl.BlockSpec((tm, d), lambda i: (i, 0)), out_shape=jax.ShapeDtypeStruct((n, d), F32),
        compiler_params=_cparams(("parallel",)), name="out_proj",
    )(x2, o_nsa_t, zn_t, od_t, zd_t, *consts)


def _mix_out(x, o_nsa, zn, od, zd, g64_ref, dow_ref, wo_ref, lam_init):
    o1 = (o_nsa * zn.astype(F32)).astype(BF16)
    odn = od * lax.rsqrt(_dot(od * od, g64_ref[...]) + NORM_EPS) * dow_ref[...] * (1.0 - lam_init)
    o2 = (odn * zd.astype(F32)).astype(BF16)
    return x + _dot(o1, wo_ref[0:NSA_WIDTH]) + _dot(o2, wo_ref[NSA_WIDTH:NSA_WIDTH + DIFF_WIDTH])


def _sample_cmp_kernel(pt_ref, q_ref, *rest, pps, past, n_sel):
    pages = rest[:pps]
    (w1_ref, posl_ref, w2_ref, g64_ref, kcw_ref, cc_ref, sac_ref, sbc_ref, ov_ref, ocmp_ref, idx_ref,
     xs_ref) = rest[pps:]
    s = pl.program_id(1)
    for k in range(pps):
        row0 = pl.multiple_of((s * pps + k) * LANES, LANES)
        for c in range(2):
            xs_ref[c, pl.ds(row0, LANES), :] = pages[k][0, c * 128:(c + 1) * 128, :].T

    @pl.when(s == pl.num_programs(1) - 1)
    def _():
        nu = past // CMP_STRIDE
        nuc = min(nu, 256)
        outs = []
        for c in range(2):
            pq = jnp.concatenate([_compress_units(xs_ref, nu, u0, nuc, w1_ref, posl_ref, w2_ref, c)
                                  for u0 in range(0, nu, nuc)], axis=0)
            outs.append(_compress_finish(pq, posl_ref, w1_ref, w2_ref, c))
        kct = _kc_norm_rope(outs[0], g64_ref, kcw_ref, cc_ref, sac_ref, sbc_ref).T.astype(BF16)
        vc = outs[1].astype(BF16)
        q = q_ref[0]
        row = lax.broadcasted_iota(jnp.int32, (NSA_HEADS, 1), 0)
        first = row < NSA_GROUP
        sc = jnp.where(first, _dot(q, kct[0:HEAD_DIM]), _dot(q, kct[HEAD_DIM:2 * HEAD_DIM]))
        cend = lax.broadcasted_iota(jnp.int32, (1, nu), 1) * CMP_STRIDE + (CMP_BLOCK - 1)
        p = _softmax_rows(sc, cend <= past)
        pb = p.astype(BF16)
        ocmp_ref[0] = jnp.where(first, _dot(pb, vc[:, 0:HEAD_DIM]), _dot(pb, vc[:, HEAD_DIM:2 * HEAD_DIM]))
        psum = jnp.concatenate([jnp.sum(p[0:NSA_GROUP], axis=0, keepdims=True),
                                jnp.sum(p[NSA_GROUP:], axis=0, keepdims=True),
                                jnp.zeros((NSA_HEADS - 2, nu), F32)], axis=0)
        hi = psum.astype(BF16)
        lo = (psum - hi.astype(F32)).astype(BF16)
        imp = _dot(hi, ov_ref[...]) + _dot(lo, ov_ref[...])
        nselp = ov_ref.shape[1]
        blk = lax.broadcasted_iota(jnp.int32, (NSA_HEADS, nselp), 1)
        cur = past // SEL_BLOCK
        forced = (blk == 0) | (blk == cur) | (blk == cur - 1)
        score = jnp.where(forced, FORCE_SCORE, jnp.where(blk * SEL_BLOCK <= past, imp, NEG_INF))
        score = jnp.where(blk < n_sel, score, EXCLUDED)
        blkf = blk.astype(F32)
        lane = lax.broadcasted_iota(jnp.int32, (NSA_HEADS, LANES), 1)
        picked = jnp.zeros((NSA_HEADS, LANES), F32)
        for k in range(min(N_SELECT, n_sel)):
            mx = jnp.max(score, axis=-1, keepdims=True)
            ik = jnp.min(jnp.where(score == mx, blkf, 1e9), axis=-1, keepdims=True)
            picked = jnp.where(lane == k, ik, picked)
            score = jnp.where(blkf == ik, EXCLUDED, score)
        idx_ref[0] = picked.astype(jnp.int32)


def _sample_cmp(page_table, q8, cache_ct, prm, cend_tables, ov, past, n_sel, pps):
    db, n_pages = page_table.shape
    n_steps = n_pages // pps
    consts = (prm['cmp_w1'], prm['cmp_posl'], prm['cmp_w2'], prm['g64_128'], prm['kcw']) + tuple(cend_tables) + (ov,)

    def full(a):
        nd = a.ndim
        return pl.BlockSpec(a.shape, lambda b, s, pt: (0,) * nd)

    def page_spec(k):
        return pl.BlockSpec((1, 256, LANES), lambda b, s, pt: (pt[b, s * pps + k], 0, 0))

    kern = functools.partial(_sample_cmp_kernel, pps=pps, past=past, n_sel=n_sel)
    return pl.pallas_call(
        kern,
        grid_spec=pltpu.PrefetchScalarGridSpec(
            num_scalar_prefetch=1, grid=(db, n_steps),
            in_specs=[pl.BlockSpec((1, NSA_HEADS, HEAD_DIM), lambda b, s, pt: (b, 0, 0))]
            + [page_spec(k) for k in range(pps)] + [full(a) for a in consts],
            out_specs=[pl.BlockSpec((1, NSA_HEADS, HEAD_DIM), lambda b, s, pt: (b, 0, 0)),
                       pl.BlockSpec((1, NSA_HEADS, LANES), lambda b, s, pt: (b, 0, 0))],
            scratch_shapes=[pltpu.VMEM((2, past, LANES), F32)]),
        out_shape=[jax.ShapeDtypeStruct((db, NSA_HEADS, HEAD_DIM), F32),
                   jax.ShapeDtypeStruct((db, NSA_HEADS, LANES), jnp.int32)],
        compiler_params=_cparams(("parallel", "arbitrary")), name="sample_cmp",
    )(page_table, q8, *([cache_ct] * pps), *consts)


def _sample_attn_kernel(pt_ref, idx_ref, q_ref, knew_ref, vnew_ref, wnew_ref, win_ref, *rest, n_k, past, n_pages):
    pages = rest[:n_k]
    osel_ref, owin_ref, wout_ref = rest[n_k:]
    b = pl.program_id(0)
    g = pl.program_id(1)
    q = q_ref[0, 0]
    lane = lax.broadcasted_iota(jnp.int32, (1, LANES), 1)
    n_cached = 2 * n_pages

    ss, vts = [], []
    for k in range(n_k):
        j = idx_ref[b, g, k]
        jc = jnp.minimum(j, n_cached - 1)
        kpos = (jc // 2) * LANES + lane
        ok = (lane // SEL_BLOCK == jc % 2) & (kpos <= past) & (j < n_cached)
        ss.append(jnp.where(ok, _dot(q, pages[k][0, 0, 0].astype(BF16)), NEG_INF))
        vts.append(pages[k][0, 1, 0].astype(BF16))
    s_new = jnp.sum(q.astype(F32) * knew_ref[0, 0][0:1], axis=-1, keepdims=True)
    m = s_new
    for s in ss:
        m = jnp.maximum(m, jnp.max(s, axis=-1, keepdims=True))
    e_new = jnp.exp(s_new - m)
    l = e_new
    acc = e_new * vnew_ref[0, 0][0:1]
    for s, vt in zip(ss, vts):
        e = jnp.exp(s - m)
        l = l + jnp.sum(e, axis=-1, keepdims=True)
        acc = acc + _dot_nt(e.astype(BF16), vt)
    osel_ref[0, 0] = acc / jnp.maximum(l, 1e-30)

    w = win_ref.shape[4]
    wl = lax.broadcasted_iota(jnp.int32, (1, w), 1)
    kt = jnp.where(wl == w - 1, wnew_ref[0, 0, 0], pltpu.roll(win_ref[0, 0, 0], w - 1, 1))
    vt = jnp.where(wl == w - 1, wnew_ref[0, 1, 0], pltpu.roll(win_ref[0, 1, 0], w - 1, 1))
    wout_ref[0, 0, 0] = kt
    wout_ref[0, 1, 0] = vt
    kpos = past - (w - 1) + wl
    d = past - kpos
    p = _softmax_rows(_dot(q, kt.astype(BF16)), (d >= 0) & (d < WINDOW) & (kpos >= 0))
    owin_ref[0, 0] = _dot_nt(p.astype(BF16), vt.astype(BF16))


def _sample_attn(page_table, idx, q4, knew, vnew, wnew, win_t, cache_st, past):
    db, n_pages = page_table.shape
    n_k = idx.shape[2]
    w = win_t.shape[4]

    def per_bg(shape):
        nd = len(shape)
        return pl.BlockSpec((1, 1) + shape, lambda b, g, pt, ix: (b, g) + (0,) * nd)

    def kv_bg(last):
        return pl.BlockSpec((1, 2, 1, HEAD_DIM, last), lambda b, g, pt, ix: (b, 0, g, 0, 0))

    def page_spec(k):
        def imap(b, g, pt, ix):
            j = jnp.minimum(ix[b, g, k], 2 * n_pages - 1)
            return (pt[b, j // 2], 0, g, 0, 0)
        return pl.BlockSpec((1, 2, 1, HEAD_DIM, LANES), imap)

    kern = functools.partial(_sample_attn_kernel, n_k=n_k, past=past, n_pages=n_pages)
    o_shape = jax.ShapeDtypeStruct((db, NSA_KV_HEADS, 8, HEAD_DIM), F32)
    return pl.pallas_call(
        kern,
        grid_spec=pltpu.PrefetchScalarGridSpec(
            num_scalar_prefetch=2, grid=(db, NSA_KV_HEADS),
            in_specs=[per_bg((8, HEAD_DIM))] * 3 + [kv_bg(1), kv_bg(w)] + [page_spec(k) for k in range(n_k)],
            out_specs=[per_bg((8, HEAD_DIM)), per_bg((8, HEAD_DIM)), kv_bg(w)]),
        out_shape=[o_shape, o_shape, jax.ShapeDtypeStruct(win_t.shape, F32)],
        compiler_params=_cparams(("parallel", "arbitrary")), name="sample_attn",
    )(page_table, idx, q4, knew, vnew, wnew, win_t, *([cache_st] * n_k))


def _sample_diff_kernel(pt_ref, qbd_ref, q16_ref, k16_ref, vnew_ref, lam_ref, *rest, pps, lam_init):
    kpages = rest[:pps]
    vpages = rest[pps:2 * pps]
    o_ref, m_scr, l_scr, acc_scr = rest[2 * pps:]
    s = pl.program_id(1)
    rows = 2 * DIFF_HEADS

    @pl.when(s == 0)
    def _():
        m_scr[...] = jnp.full(m_scr.shape, NEG_INF, F32)
        l_scr[...] = jnp.zeros(l_scr.shape, F32)
        acc_scr[...] = jnp.zeros(acc_scr.shape, F32)

    qbd = qbd_ref[0]
    ss = [_dot(qbd, kp[0].astype(BF16)) for kp in kpages]
    cm = ss[0]
    for sk in ss[1:]:
        cm = jnp.maximum(cm, sk)
    m_old = m_scr[...]
    m_new = jnp.maximum(m_old, jnp.max(cm, axis=-1, keepdims=True))
    alpha = jnp.exp(m_old - m_new)
    lsum = jnp.zeros((rows, LANES), F32)
    accs = jnp.zeros(acc_scr.shape, F32)
    for sk, vp in zip(ss, vpages):
        e = jnp.exp(sk - m_new)
        lsum = lsum + e
        accs = accs + _dot_nt(e.astype(BF16), vp[0].astype(BF16))
    m_scr[...] = m_new
    l_scr[...] = alpha * l_scr[...] + jnp.sum(lsum, axis=-1, keepdims=True)
    acc_scr[...] = alpha[:, 0:1] * acc_scr[...] + accs

    @pl.when(s == pl.num_programs(1) - 1)
    def _():
        s_new = jnp.sum(q16_ref[0] * k16_ref[0], axis=-1, keepdims=True)
        m1 = m_scr[...][:, 0:1]
        m2 = jnp.maximum(m1, s_new)
        a2 = jnp.exp(m1 - m2)
        e_new = jnp.exp(s_new - m2)
        l = a2 * l_scr[...][:, 0:1] + e_new
        acc = acc_scr[...]
        col = lax.broadcasted_iota(jnp.int32, acc.shape, 1) // DIFF_V_DIM
        rowh = lax.broadcasted_iota(jnp.int32, acc.shape, 0) % DIFF_HEADS
        acc = jnp.where(col == rowh, acc, 0.0)
        o16 = acc[:, 0:DIFF_V_DIM]
        for h in range(1, DIFF_HEADS):
            o16 = o16 + acc[:, h * DIFF_V_DIM:(h + 1) * DIFF_V_DIM]
        vnew = vnew_ref[0]
        o16 = (a2 * o16 + e_new * jnp.concatenate([vnew, vnew], axis=0)) / jnp.maximum(l, 1e-30)
        o_ref[0] = o16[0:DIFF_HEADS] - _diff_lambda(lam_ref, lam_init) * o16[DIFF_HEADS:]


def _sample_diff(page_table, qbd, q16, k16, vnew, lam_vec, cache_dkt, cache_dvt, lam_init, pps):
    db, n_pages = page_table.shape
    n_steps = n_pages // pps
    rows = 2 * DIFF_HEADS

    def per_b(shape):
        nd = len(shape)
        return pl.BlockSpec((1,) + shape, lambda b, s, pt: (b,) + (0,) * nd)

    def page_spec(k):
        return pl.BlockSpec((1, 512, LANES), lambda b, s, pt: (pt[b, s * pps + k], 0, 0))

    kern = functools.partial(_sample_diff_kernel, pps=pps, lam_init=lam_init)
    return pl.pallas_call(
        kern,
        grid_spec=pltpu.PrefetchScalarGridSpec(
            num_scalar_prefetch=1, grid=(db, n_steps),
            in_specs=[per_b((rows, 512)), per_b((rows, DIFF_QK_DIM)), per_b((rows, DIFF_QK_DIM)),
                      per_b((DIFF_HEADS, DIFF_V_DIM)), pl.BlockSpec(lam_vec.shape, lambda b, s, pt: (0, 0))]
            + [page_spec(k) for k in range(pps)] * 2,
            out_specs=per_b((DIFF_HEADS, DIFF_V_DIM)),
            scratch_shapes=[pltpu.VMEM((rows, LANES), F32), pltpu.VMEM((rows, LANES), F32),
                            pltpu.VMEM((rows, 512), F32)]),
        out_shape=jax.ShapeDtypeStruct((db, DIFF_HEADS, DIFF_V_DIM), F32),
        compiler_params=_cparams(("parallel", "arbitrary")), name="sample_diff",
    )(page_table, qbd, q16, k16, vnew, lam_vec, *([cache_dkt] * pps), *([cache_dvt] * pps))


def _sample_out_kernel(x_ref, oc_ref, os_ref, ow_ref, g0_ref, g1_ref, g2_ref, zn_ref, od_ref, zd_ref, g64_ref, dow_ref,
                       wo_ref, y_ref, *, lam_init):
    o_nsa = oc_ref[...] * g0_ref[...] + os_ref[...] * g1_ref[...] + ow_ref[...] * g2_ref[...]
    y_ref[...] = _mix_out(x_ref[...], o_nsa, zn_ref[...], od_ref[...], zd_ref[...], g64_ref, dow_ref, wo_ref, lam_init)


def _sample_out(x2, branches, gates, zn, od, zd, prm, lam_init):
    args = (x2,) + tuple(branches) + tuple(gates) + (zn, od, zd, prm['g64'], prm['dow'], prm['w_out'])
    return pl.pallas_call(
        functools.partial(_sample_out_kernel, lam_init=lam_init),
        out_shape=jax.ShapeDtypeStruct(x2.shape, F32), compiler_params=_cparams(None), name="sample_out",
    )(*args)


def _layer_params(lp):
    w = lp['w_in']
    d = w.shape[0]
    cols = [w[:, _OFFS[k]:_OFFS[k + 1]] for k in _ROW_ORDER]
    cols.append(jnp.zeros((d, GATE_ROWS - _SIZES[4]), F32))
    w_t = jnp.concatenate(cols, axis=1).T

    def cmp_w1(w1):
        w1 = w1.reshape(2, CMP_STRIDE, HEAD_DIM, HEAD_DIM)
        eye = jnp.eye(NSA_KV_HEADS, dtype=F32)
        full = jnp.einsum('hlde,gk->lgdhke', w1, eye)
        return full.reshape(CMP_STRIDE * NSA_KV_HEADS * HEAD_DIM, 2 * NSA_KV_HEADS * HEAD_DIM)

    def cmp_pos(pos):
        pos = pos.reshape(2, CMP_STRIDE, 1, HEAD_DIM)
        rows = jnp.broadcast_to(pos, (2, CMP_STRIDE, NSA_KV_HEADS, HEAD_DIM)).reshape(2, -1)
        return jnp.concatenate([rows, jnp.zeros((6, rows.shape[1]), F32)], axis=0)

    def cmp_w2(w2):
        return jnp.kron(jnp.eye(NSA_KV_HEADS, dtype=F32), w2)

    lam_vec = jnp.concatenate([lp['lambda_q1'][None], lp['lambda_k1'][None], lp['lambda_q2'][None],
                               lp['lambda_k2'][None], jnp.zeros((4, DIFF_QK_DIM), F32)], axis=0)
    return {
        'norm_w': lp['norm_w'][None], 'w_t': w_t.astype(BF16),
        'g64': _group_mean_matrix(512, HEAD_DIM), 'g64_128': _group_mean_matrix(LANES, HEAD_DIM),
        'qnw': lp['nsa_q_norm'][:, None], 'qdw': lp['diff_q_norm'][:, None],
        'ksw': lp['nsa_ks_norm'][:, None], 'kww': lp['nsa_kw_norm'][:, None], 'kdw': lp['diff_k_norm'][:, None],
        'kcw': jnp.tile(lp['nsa_kc_norm'], NSA_KV_HEADS)[None],
        'cmp_w1': jnp.stack([cmp_w1(lp['cmp_w1_k']), cmp_w1(lp['cmp_w1_v'])]).astype(BF16),
        'cmp_posl': jnp.stack([cmp_pos(lp['cmp_pos_k']), cmp_pos(lp['cmp_pos_v'])]).astype(BF16),
        'cmp_w2': jnp.stack([cmp_w2(lp['cmp_w2_k']), cmp_w2(lp['cmp_w2_v'])]).astype(BF16),
        'dow': jnp.tile(lp['diff_out_norm'], DIFF_HEADS)[None], 'dow_col': lp['diff_out_norm'][:, None],
        'w_out': lp['w_out'].astype(BF16), 'lam_vec': lam_vec,
    }


def _overlap_t(n_sel_pad, nu, n_cmp, n_sel):
    cs = np.arange(nu)[None, :] * CMP_STRIDE
    ss = np.arange(n_sel_pad)[:, None] * SEL_BLOCK
    ov = np.clip(np.minimum(cs + CMP_BLOCK, ss + SEL_BLOCK) - np.maximum(cs, ss), 0, None) / CMP_STRIDE
    ov = ov * (np.arange(nu)[None, :] < n_cmp) * (np.arange(n_sel_pad)[:, None] < n_sel)
    return jnp.asarray(ov, dtype=BF16)


def _prompt_layer(x, prm, lam_init):
    b, t, d = x.shape
    x2 = x.reshape(b * t, d)
    tables = _rope_tables(jnp.arange(t))
    qn_t, kvc_t, kvs_t, kvw_t, zn_t, qd_t, kd_t, vd_t, zd_t, g_t = _project(x2, tables, t, prm, 512, True)
    nu = t // CMP_STRIDE
    n_cmp = nu - CMP_BLOCK // CMP_STRIDE + 1
    cend_tables = _rope_tables(jnp.arange(nu) * CMP_STRIDE + CMP_BLOCK - 1)[0]
    kc, vct = _compress_prompt(kvc_t, prm, cend_tables)
    n_sel = -(-t // SEL_BLOCK)
    ovt = _overlap_t(n_sel, nu, n_cmp, n_sel)
    expand_t = jnp.asarray(np.arange(t)[:, None] // SEL_BLOCK == np.arange(n_sel)[None, :], dtype=BF16)
    o_nsa_t = _nsa_prompt(qn_t, g_t, kc, vct, kvs_t, kvw_t, ovt, expand_t, 512)
    o_d_t = _diff_prompt(qd_t, kd_t, vd_t, prm['lam_vec'], lam_init, 512)
    y = _out_proj(x2, o_nsa_t, zn_t, o_d_t, zd_t, prm, lam_init, 512).reshape(b, t, d)

    def state(a_t, shape_tail):
        return jnp.transpose(a_t.reshape((b,) + shape_tail + (a_t.shape[-1],)),
                             (0, len(shape_tail) + 1) + tuple(range(1, len(shape_tail) + 1)))

    kv_tail = (2, NSA_KV_HEADS, HEAD_DIM)
    w = min(WINDOW, t)
    return y, (state(kvc_t, kv_tail), state(kvs_t, kv_tail), state(kvw_t[:, :, t - w:], kv_tail),
               state(kd_t, (DIFF_HEADS, 2, DIFF_QK_DIM)), state(vd_t, (DIFF_HEADS, DIFF_V_DIM)))


def _sample_layer(x, cache_c, cache_s, cache_dk, cache_dv, win, page_table, prm, lam_init):
    db, t, d = x.shape
    assert t == 1 and db % 8 == 0
    n_pool, page = cache_c.shape[:2]
    assert page == LANES
    n_pages = page_table.shape[1]
    past = n_pages * page
    x2 = x.reshape(db, d)
    dbp = -(-db // LANES) * LANES
    tables = _rope_tables(jnp.full((dbp,), past))
    proj = _project(jnp.pad(x2, ((0, dbp - db), (0, 0))), tables, dbp, prm, dbp, False)
    qn, kvc, kvs, kvw, zn, qd, kd, vd, zd, gn = [a[0, :, :db].T for a in proj]

    cache_ct = jnp.transpose(cache_c, (0, 2, 3, 4, 1)).reshape(n_pool, 256, page)
    cache_st = jnp.transpose(cache_s, (0, 2, 3, 4, 1))
    cache_dkt = jnp.transpose(cache_dk, (0, 2, 3, 4, 1)).reshape(n_pool, 512, page)
    cache_dvt = jnp.transpose(cache_dv, (0, 2, 3, 1)).reshape(n_pool, 512, page)
    win_t = jnp.transpose(win, (0, 2, 3, 4, 1))

    nu = past // CMP_STRIDE
    n_cmp = nu - CMP_BLOCK // CMP_STRIDE + 1
    n_sel = -(-(past + t) // SEL_BLOCK)
    n_sel_pad = -(-n_sel // LANES) * LANES
    ov = _overlap_t(n_sel_pad, nu, n_cmp, n_sel).T
    cend_tables = _rope_tables(jnp.arange(nu) * CMP_STRIDE + CMP_BLOCK - 1)[0]
    pps = min(16, n_pages)
    o_cmp, picked = _sample_cmp(page_table, qn.reshape(db, NSA_HEADS, HEAD_DIM), cache_ct, prm, cend_tables, ov, past,
                                n_sel, pps)
    n_k = min(N_SELECT, n_sel)
    idx = picked[:, :NSA_KV_HEADS, :n_k]

    def rows8(a):
        return jnp.pad(a, ((0, 0), (0, 0), (0, 8 - a.shape[2]), (0, 0)))

    kvs5 = kvs.reshape(db, 2, NSA_KV_HEADS, 1, HEAD_DIM)
    q4 = rows8(qn.reshape(db, NSA_KV_HEADS, NSA_GROUP, HEAD_DIM))
    o_sel, o_win, win_new = _sample_attn(page_table, idx, q4, rows8(kvs5[:, 0]), rows8(kvs5[:, 1]),
                                         kvw.reshape(db, 2, NSA_KV_HEADS, HEAD_DIM, 1), win_t, cache_st, past)

    qd3 = qd.astype(F32).reshape(db, DIFF_HEADS, 2, DIFF_QK_DIM)
    q16 = jnp.transpose(qd3, (0, 2, 1, 3)).reshape(db, 2 * DIFF_HEADS, DIFF_QK_DIM)
    own = np.zeros((2 * DIFF_HEADS, DIFF_QK_COLS), np.float32)
    for c in range(2):
        for h in range(DIFF_HEADS):
            lo = h * 2 * DIFF_QK_DIM + c * DIFF_QK_DIM
            own[c * DIFF_HEADS + h, lo:lo + DIFF_QK_DIM] = 1.0
    qbd = (qd.astype(F32)[:, None, :] * own[None]).astype(BF16)
    kd3 = kd.reshape(db, DIFF_HEADS, 2, DIFF_QK_DIM)
    k16 = jnp.transpose(kd3, (0, 2, 1, 3)).reshape(db, 2 * DIFF_HEADS, DIFF_QK_DIM)
    vnew = vd.reshape(db, DIFF_HEADS, DIFF_V_DIM)
    o_d = _sample_diff(page_table, qbd, q16, k16, vnew, prm['lam_vec'], cache_dkt, cache_dvt, lam_init,
                       min(16, n_pages))

    gates = [jnp.repeat(gn[:, br * NSA_HEADS:(br + 1) * NSA_HEADS], HEAD_DIM, axis=1) for br in range(3)]
    branches = (o_cmp.reshape(db, NSA_WIDTH), o_sel[:, :, :NSA_GROUP].reshape(db, NSA_WIDTH),
                o_win[:, :, :NSA_GROUP].reshape(db, NSA_WIDTH))
    y = _sample_out(x2, branches, gates, zn, o_d.reshape(db, DIFF_WIDTH), zd, prm, lam_init).reshape(db, t, d)

    kv_shape = (db, t, 2, NSA_KV_HEADS, HEAD_DIM)
    states = (kvc.reshape(kv_shape), kvs.reshape(kv_shape), jnp.transpose(win_new, (0, 4, 1, 2, 3)),
              kd3.reshape(db, t, DIFF_HEADS, 2, DIFF_QK_DIM), vnew.reshape(db, t, DIFF_HEADS, DIFF_V_DIM))
    return y, states


def kernel(x_prompt, x_sample, cache_nsa_cmp_kv, cache_nsa_sel_kv, cache_diff_k, cache_diff_v, state_nsa_win_kv, page_table, norm_w, w_in, nsa_q_norm, nsa_kc_norm, nsa_ks_norm, nsa_kw_norm, cmp_pos_k, cmp_w1_k, cmp_w2_k, cmp_pos_v, cmp_w1_v, cmp_w2_v, diff_q_norm, diff_k_norm, lambda_q1, lambda_k1, lambda_q2, lambda_k2, diff_out_norm, w_out):
    depth = w_in.shape[0]
    y_p, y_s = x_prompt, x_sample
    p_states, s_states = [], []
    for layer in range(depth):
        lp = {
            'norm_w': norm_w[layer], 'w_in': w_in[layer],
            'nsa_q_norm': nsa_q_norm[layer], 'nsa_kc_norm': nsa_kc_norm[layer],
            'nsa_ks_norm': nsa_ks_norm[layer], 'nsa_kw_norm': nsa_kw_norm[layer],
            'cmp_pos_k': cmp_pos_k[layer], 'cmp_w1_k': cmp_w1_k[layer], 'cmp_w2_k': cmp_w2_k[layer],
            'cmp_pos_v': cmp_pos_v[layer], 'cmp_w1_v': cmp_w1_v[layer], 'cmp_w2_v': cmp_w2_v[layer],
            'diff_q_norm': diff_q_norm[layer], 'diff_k_norm': diff_k_norm[layer],
            'lambda_q1': lambda_q1[layer], 'lambda_k1': lambda_k1[layer],
            'lambda_q2': lambda_q2[layer], 'lambda_k2': lambda_k2[layer],
            'diff_out_norm': diff_out_norm[layer], 'w_out': w_out[layer],
        }
        lam_init = 0.8 - 0.6 * math.exp(-0.3 * layer)
        prm = _layer_params(lp)
        y_p, ps = _prompt_layer(y_p, prm, lam_init)
        y_s, ss = _sample_layer(y_s, cache_nsa_cmp_kv[layer], cache_nsa_sel_kv[layer], cache_diff_k[layer],
                                cache_diff_v[layer], state_nsa_win_kv[layer], page_table, prm, lam_init)
        p_states.append(ps)
        s_states.append(ss)
    p_c, p_s, p_w, p_dk, p_dv = [jnp.stack(t, axis=0) for t in zip(*p_states)]
    s_c, s_s, s_w, s_dk, s_dv = [jnp.stack(t, axis=0) for t in zip(*s_states)]
    return (y_p, y_s, p_c, p_s, p_w, p_dk, p_dv, s_c, s_s, s_w, s_dk, s_dv)
```

```python
import functools
import math

import jax
import jax.numpy as jnp
import numpy as np
from jax import lax
from jax.experimental import pallas as pl
from jax.experimental.pallas import tpu as pltpu

HEAD_DIM = 64
NSA_HEADS = 8
NSA_KV_HEADS = 2
NSA_GROUP = NSA_HEADS // NSA_KV_HEADS
NSA_WIDTH = NSA_HEADS * HEAD_DIM
CMP_BLOCK = 32
CMP_STRIDE = 16
SEL_BLOCK = 64
N_SELECT = 16
WINDOW = 512
DIFF_HEADS = 8
DIFF_QK_DIM = 32
DIFF_V_DIM = 64
DIFF_WIDTH = DIFF_HEADS * DIFF_V_DIM
KV_COLS = 2 * NSA_KV_HEADS * HEAD_DIM
DIFF_QK_COLS = DIFF_HEADS * 2 * DIFF_QK_DIM
ROPE_THETA = 500000.0
ROT_FRACTION = 4
NORM_EPS = 1e-6
NEG_INF = -1e30
FORCE_SCORE = 1e9
EXCLUDED = -3e38
LOG2E = 1.4426950408889634

LANES = 128
BF16_ROWS = 16
KEY_CHUNK = 512
VMEM_LIMIT = 56 * 1024 * 1024

F32 = jnp.float32
BF16 = jnp.bfloat16

_SIZES = (NSA_WIDTH, KV_COLS, KV_COLS, KV_COLS, 3 * NSA_HEADS, NSA_WIDTH, DIFF_QK_COLS, DIFF_QK_COLS, DIFF_WIDTH,
          DIFF_WIDTH)
_OFFS = tuple(int(v) for v in np.concatenate([[0], np.cumsum(_SIZES)]))
_ROW_ORDER = (0, 1, 2, 3, 5, 6, 7, 8, 9, 4)
GATE_ROWS = 32
_ROW_SIZES = tuple(_SIZES[k] for k in _ROW_ORDER[:-1]) + (GATE_ROWS,)
_ROW_OFFS = tuple(int(v) for v in np.concatenate([[0], np.cumsum(_ROW_SIZES)]))
PROJ_ROWS = _ROW_OFFS[-1]


def _cparams(sem):
    return pltpu.CompilerParams(dimension_semantics=sem, vmem_limit_bytes=VMEM_LIMIT)


def _sigmoid(x):
    return 1.0 / (1.0 + jnp.exp(-x))


def _dot(a, b):
    return jnp.dot(a, b, preferred_element_type=F32)


def _dot_nt(a, b):
    return lax.dot_general(a, b, (((1,), (1,)), ((), ())), preferred_element_type=F32)


def _rope_lanes(x, c, sa, sb, half):
    parts = []
    for k in range(x.shape[1] // LANES):
        xk = x[:, LANES * k:LANES * (k + 1)]
        parts.append(xk * c + pltpu.roll(xk, LANES - half, 1) * sa + pltpu.roll(xk, half, 1) * sb)
    return jnp.concatenate(parts, axis=1) if len(parts) > 1 else parts[0]


def _norm_rope_rows(x, w_ref, c, s, groups, gd):
    tm = x.shape[1]
    x3 = x.reshape(groups, gd, tm)
    r = lax.rsqrt(jnp.mean(x3 * x3, axis=1, keepdims=True) + NORM_EPS)
    x3 = (x3 * r) * w_ref[...].reshape(1, gd, 1)
    if gd == HEAD_DIM:
        x1 = x3[:, 0:8]
        x2 = x3[:, 8:16]
        parts = [x1 * c - x2 * s, x2 * c + x1 * s, x3[:, 16:]]
    else:
        rot = x3[:, 0:8].reshape(groups * 8, tm)
        first = lax.broadcasted_iota(jnp.int32, rot.shape, 0) % 8 < 4
        swapped = jnp.where(first, pltpu.roll(rot, rot.shape[0] - 4, 0), pltpu.roll(rot, 4, 0))
        parts = [rot.reshape(groups, 8, tm) * c + swapped.reshape(groups, 8, tm) * s, x3[:, 8:]]
    return jnp.concatenate(parts, axis=1).reshape(groups * gd, tm)


def _proj_kernel(x_ref, nw_ref, wt_ref, qnw_ref, ksw_ref, kww_ref, qdw_ref, kdw_ref, c8_ref, s8_ref, c4_ref, s4_ref,
                 qn_ref, kvc_ref, kvs_ref, kvw_ref, zn_ref, qd_ref, kd_ref, vd_ref, zd_ref, g_ref, *, qn_scale,
                 qd_scale):
    x = x_ref[...]
    ms = jnp.mean(x * x, axis=-1, keepdims=True)
    ht = ((x * lax.rsqrt(ms + NORM_EPS)) * nw_ref[...]).T.astype(BF16)

    def rows(k):
        return _dot(wt_ref[_ROW_OFFS[k]:_ROW_OFFS[k + 1]], ht)

    c8, s8, c4, s4 = c8_ref[...], s8_ref[...], c4_ref[...], s4_ref[...]
    qn_ref[0] = (_norm_rope_rows(rows(0), qnw_ref, c8, s8, NSA_HEADS, HEAD_DIM) * qn_scale).astype(BF16)
    kvc_ref[0] = rows(1)
    kvs = rows(2)
    kvs_ref[0, 0:128] = _norm_rope_rows(kvs[0:128], ksw_ref, c8, s8, NSA_KV_HEADS, HEAD_DIM)
    kvs_ref[0, 128:256] = kvs[128:256]
    kvw = rows(3)
    kvw_ref[0, 0:128] = _norm_rope_rows(kvw[0:128], kww_ref, c8, s8, NSA_KV_HEADS, HEAD_DIM)
    kvw_ref[0, 128:256] = kvw[128:256]
    zn = rows(4)
    zn_ref[0] = (zn * _sigmoid(zn)).astype(BF16)
    qd_ref[0] = (_norm_rope_rows(rows(5), qdw_ref, c4, s4, 2 * DIFF_HEADS, DIFF_QK_DIM) * qd_scale).astype(BF16)
    kd_ref[0] = _norm_rope_rows(rows(6), kdw_ref, c4, s4, 2 * DIFF_HEADS, DIFF_QK_DIM)
    vd_ref[0] = rows(7)
    zd = rows(8)
    zd_ref[0] = (zd * _sigmoid(zd)).astype(BF16)
    g_ref[0] = _sigmoid(rows(9))


def _rope_tables(pos):
    pos = np.asarray(pos, np.float32)
    lane = np.arange(LANES)

    def tables(group, half):
        inv = np.float32(ROPE_THETA) ** (-np.arange(half, dtype=np.float32) / np.float32(half))
        ang = (pos[:, None] * inv).astype(np.float32)
        cos, sin = np.cos(ang), np.sin(ang)
        m = lane % group
        cosl, sinl = cos[:, m % half], sin[:, m % half]
        c = np.where(m < 2 * half, cosl, 1.0)
        sa = np.where(m < half, -sinl, 0.0)
        sb = np.where((m >= half) & (m < 2 * half), sinl, 0.0)
        return tuple(jnp.asarray(a, F32) for a in (c, sa, sb)), (cos.T, sin.T)

    tq, (c8, s8) = tables(HEAD_DIM, HEAD_DIM // ROT_FRACTION // 2)
    td, (c4, s4) = tables(DIFF_QK_DIM, DIFF_QK_DIM // ROT_FRACTION // 2)
    c4 = np.concatenate([c4, c4], axis=0)
    s4 = np.concatenate([-s4, s4], axis=0)
    return tq, td, tuple(jnp.asarray(a, F32) for a in (c8, s8)), tuple(jnp.asarray(a, F32) for a in (c4, s4))


def _group_mean_matrix(width, group):
    idx = np.arange(width)
    return jnp.asarray((idx[:, None] // group == idx[None, :] // group) / group, dtype=F32)


def _project(x2, pos_tables, t_per_batch, prm, tm, log2e_in_q):
    n, d = x2.shape
    nb = n // t_per_batch
    nt = t_per_batch // tm
    (c8, s8), (c4, s4) = pos_tables[2], pos_tables[3]
    fold = LOG2E if log2e_in_q else 1.0

    def full(a):
        return pl.BlockSpec(a.shape, lambda i: (0, 0))

    def out(i):
        return (i // nt, 0, i % nt)

    consts = (prm['norm_w'], prm['w_t'], prm['qnw'], prm['ksw'], prm['kww'], prm['qdw'], prm['kdw'])
    dts = (BF16, F32, F32, F32, BF16, BF16, F32, F32, BF16, F32)
    kern = functools.partial(_proj_kernel, qn_scale=HEAD_DIM ** -0.5 * fold, qd_scale=DIFF_QK_DIM ** -0.5 * fold)
    return pl.pallas_call(
        kern, grid=(n // tm,),
        in_specs=[pl.BlockSpec((tm, d), lambda i: (i, 0))] + [full(a) for a in consts]
        + [pl.BlockSpec((8, tm), lambda i: (0, i % nt))] * 4,
        out_specs=[pl.BlockSpec((1, r, tm), out) for r in _ROW_SIZES],
        out_shape=[jax.ShapeDtypeStruct((nb, r, t_per_batch), dt) for r, dt in zip(_ROW_SIZES, dts)],
        compiler_params=_cparams(("parallel",)), name="proj",
    )(x2, *consts, c8, s8, c4, s4)


def _compress_units(xs_ref, nu, u0, nuc, w1_ref, posl_ref, w2_ref, c):
    lhs = jnp.concatenate(
        [xs_ref[c, pl.ds(u0 * CMP_STRIDE + l, nuc, stride=CMP_STRIDE), :] for l in range(CMP_STRIDE)],
        axis=1).astype(BF16)
    return _dot(lhs, w1_ref[c])


def _compress_finish(pq, posl_ref, w1_ref, w2_ref, c):
    nu = pq.shape[0]
    pb = _dot(posl_ref[c], w1_ref[c])
    bias = pb[0:1, 0:128] + pb[1:2, 128:256]
    hid = pq[:, 0:128] + pltpu.roll(pq[:, 128:256], nu - 1, 0) + bias
    hid = hid * _sigmoid(hid)
    return _dot(hid.astype(BF16), w2_ref[c])


def _kc_norm_rope(kc, g64_ref, kcw_ref, cc_ref, sac_ref, sbc_ref):
    kc = kc * lax.rsqrt(_dot(kc * kc, g64_ref[...]) + NORM_EPS) * kcw_ref[...]
    return _rope_lanes(kc, cc_ref[...], sac_ref[...], sbc_ref[...], 8)


def _cmp_prompt_kernel(kvc_ref, w1_ref, posl_ref, w2_ref, g64_ref, kcw_ref, cc_ref, sac_ref, sbc_ref,
                       kc_ref, vct_ref, xs_ref):
    t = kvc_ref.shape[2]
    nu = t // CMP_STRIDE
    for c in range(2):
        xs_ref[c] = kvc_ref[0, c * 128:(c + 1) * 128, :].T
    outs = []
    for c in range(2):
        pq = _compress_units(xs_ref, nu, 0, nu, w1_ref, posl_ref, w2_ref, c)
        outs.append(_compress_finish(pq, posl_ref, w1_ref, w2_ref, c))
    kc = _kc_norm_rope(outs[0], g64_ref, kcw_ref, cc_ref, sac_ref, sbc_ref)
    vct = outs[1].T
    for g in range(NSA_KV_HEADS):
        kc_ref[0, g] = kc[:, g * HEAD_DIM:(g + 1) * HEAD_DIM]
        vct_ref[0, g] = vct[g * HEAD_DIM:(g + 1) * HEAD_DIM]


def _compress_prompt(kvc_t, prm, cend_tables):
    nb, _, t = kvc_t.shape
    nu = t // CMP_STRIDE
    cc, sac, sbc = cend_tables

    def full(a):
        nd = a.ndim
        return pl.BlockSpec(a.shape, lambda b: (0,) * nd)

    consts = (prm['cmp_w1'], prm['cmp_posl'], prm['cmp_w2'], prm['g64_128'], prm['kcw'], cc, sac, sbc)
    return pl.pallas_call(
        _cmp_prompt_kernel, grid=(nb,),
        in_specs=[pl.BlockSpec((1, 256, t), lambda b: (b, 0, 0))] + [full(a) for a in consts],
        out_specs=[pl.BlockSpec((1, NSA_KV_HEADS, nu, HEAD_DIM), lambda b: (b, 0, 0, 0)),
                   pl.BlockSpec((1, NSA_KV_HEADS, HEAD_DIM, nu), lambda b: (b, 0, 0, 0))],
        out_shape=[jax.ShapeDtypeStruct((nb, NSA_KV_HEADS, nu, HEAD_DIM), F32),
                   jax.ShapeDtypeStruct((nb, NSA_KV_HEADS, HEAD_DIM, nu), F32)],
        scratch_shapes=[pltpu.VMEM((2, t, LANES), F32)],
        compiler_params=_cparams(("parallel",)), name="cmp_prompt",
    )(kvc_t, *consts)


def _softmax_rows(s, mask):
    sm = jnp.where(mask, s, NEG_INF)
    m = jnp.max(sm, axis=-1, keepdims=True)
    e = jnp.where(mask, jnp.exp(sm - m), 0.0)
    return e / jnp.maximum(jnp.sum(e, axis=-1, keepdims=True), 1e-30)


def _ones_rows(t):
    return jnp.where(lax.broadcasted_iota(jnp.int32, (BF16_ROWS, t), 0) == 0, 1.0, 0.0).astype(BF16)


def _flash_init(m_scr, acc_scr):
    m_scr[...] = jnp.full(m_scr.shape, NEG_INF, F32)
    acc_scr[...] = jnp.zeros(acc_scr.shape, F32)


def _flash_step(problems, off, nk, lanes, m_scr, acc_scr):
    lo, hi = lanes
    for n, qt, k_ref, v_ref, bias_fn in problems:
        s = _dot(k_ref[pl.ds(off, nk), :], qt[:, lo:hi]) + bias_fn(off, nk)[:, lo:hi]
        m_old = m_scr[n, 0:1, lo:hi]
        m_new = jnp.maximum(m_old, jnp.max(s, axis=0, keepdims=True))
        p = jnp.exp2(s - m_new).astype(BF16)
        acc_scr[n, :, lo:hi] = jnp.exp2(m_old - m_new) * acc_scr[n, :, lo:hi] + _dot(v_ref[:, pl.ds(off, nk)], p)
        m_scr[n, :, lo:hi] = jnp.broadcast_to(m_new, (m_scr.shape[1], hi - lo))


def _flash_run(problems, c_lo, c_hi, m_scr, acc_scr):
    r = m_scr.shape[2]

    def body(c, carry):
        _flash_step(problems, pl.multiple_of(c * KEY_CHUNK, KEY_CHUNK), KEY_CHUNK, (0, r), m_scr, acc_scr)
        return carry

    lax.fori_loop(c_lo, c_hi, body, 0)


def _flash_finish(n, m_scr, acc_scr):
    dv = acc_scr.shape[1] - BF16_ROWS
    acc = acc_scr[n]
    return jnp.where(m_scr[n, 0:1] > 0.5 * NEG_INF, acc[0:dv] / jnp.maximum(acc[dv:dv + 1], 1e-30), 0.0)


def _nsa_prompt_kernel(q_ref, g_ref, kc_ref, vct_ref, kst_ref, vst_ref, kwt_ref, vwt_ref, ovt_ref, exp_ref, o_ref,
                       ks_sd, kw_sd, vs_aug, vw_aug, m_scr, acc_scr, *, tq):
    grp = pl.program_id(1)
    i = pl.program_id(2)
    t = vst_ref.shape[2]

    @pl.when(i == 0)
    def _():
        def group_cols(kt_ref):
            kk = kt_ref[0].T
            return jnp.where(grp == 0, kk[:, 0:HEAD_DIM], kk[:, HEAD_DIM:2 * HEAD_DIM]).astype(BF16)

        ks_sd[...] = group_cols(kst_ref)
        kw_sd[...] = group_cols(kwt_ref)
        vs_aug[0:HEAD_DIM] = vst_ref[0].astype(BF16)
        vs_aug[HEAD_DIM:HEAD_DIM + BF16_ROWS] = _ones_rows(t)
        vw_aug[0:HEAD_DIM] = vwt_ref[0].astype(BF16)
        vw_aug[HEAD_DIM:HEAD_DIM + BF16_ROWS] = _ones_rows(t)

    nu = kc_ref.shape[2]
    nsel = ovt_ref.shape[0]
    assert tq == KEY_CHUNK == WINDOW
    half = tq // 2
    r = NSA_GROUP * tq
    qb = q_ref[0]
    qt = jnp.concatenate([qb[HEAD_DIM * h:HEAD_DIM * (h + 1), a * half:(a + 1) * half]
                          for a in range(2) for h in range(NSA_GROUP)], axis=1)

    def per_head(x):
        return jnp.concatenate([x[:, a * half:(a + 1) * half] for a in range(2) for _ in range(NSA_GROUP)], axis=1)

    def head_lanes(x, h):
        return jnp.concatenate([x[:, (a * NSA_GROUP + h) * half:(a * NSA_GROUP + h + 1) * half] for a in range(2)],
                               axis=1)

    t0 = i * tq
    tpos = t0 + lax.broadcasted_iota(jnp.int32, (1, tq), 1)
    qpos = per_head(tpos)

    s = _dot(kc_ref[0, 0].astype(BF16), qt)
    cend = lax.broadcasted_iota(jnp.int32, (nu, 1), 0) * CMP_STRIDE + (CMP_BLOCK - 1)
    vis = cend <= qpos
    sm = jnp.where(vis, s, NEG_INF)
    e = jnp.where(vis, jnp.exp2(sm - jnp.max(sm, axis=0, keepdims=True)), 0.0)
    p = e / jnp.maximum(jnp.sum(e, axis=0, keepdims=True), 1e-30)
    o_cmp = _dot(vct_ref[0, 0].astype(BF16), p.astype(BF16))

    psum = head_lanes(p, 0)
    for h in range(1, NSA_GROUP):
        psum = psum + head_lanes(p, h)
    hi = psum.astype(BF16)
    lo = (psum - hi.astype(F32)).astype(BF16)
    imp = _dot(ovt_ref[...], hi) + _dot(ovt_ref[...], lo)
    blk = lax.broadcasted_iota(jnp.int32, (nsel, tq), 0)
    tl = t0 + lax.broadcasted_iota(jnp.int32, (nsel, tq), 1)
    cur = tl // SEL_BLOCK
    forced = (blk == 0) | (blk == cur) | (blk == cur - 1)
    score = jnp.where(forced, FORCE_SCORE, jnp.where(blk * SEL_BLOCK <= tl, imp, NEG_INF))
    cnt = jnp.zeros((nsel, tq), jnp.int32)
    for jp in range(nsel):
        other = score[jp:jp + 1, :]
        ahead = (other > score) | ((other == score) & (blk > jp))
        cnt = cnt + jnp.where(ahead, 1, 0)
    sel_bias = jnp.where(cnt < min(N_SELECT, nsel), 0.0, NEG_INF).astype(BF16)

    def kpos_of(off, nk):
        return off + lax.broadcasted_iota(jnp.int32, (nk, 1), 0)

    def sel_mask(off, nk):
        bias = _dot(exp_ref[pl.ds(off, nk), :], sel_bias)
        return per_head(jnp.where(kpos_of(off, nk) <= tpos, bias, NEG_INF))

    def win_mask(off, nk):
        d = tpos - kpos_of(off, nk)
        return per_head(jnp.where((d >= 0) & (d < WINDOW), 0.0, NEG_INF))

    sel_p = (0, qt, ks_sd, vs_aug, sel_mask)
    win_p = (1, qt, kw_sd, vw_aug, win_mask)
    _flash_init(m_scr, acc_scr)
    c_prev = jnp.maximum(i - 1, 0)
    _flash_run([sel_p], 0, c_prev, m_scr, acc_scr)

    def prev_chunk(c, carry):
        off = pl.multiple_of(c * KEY_CHUNK, KEY_CHUNK)
        _flash_step([sel_p], off, KEY_CHUNK, (0, r), m_scr, acc_scr)
        _flash_step([win_p], off, half, (0, r // 2), m_scr, acc_scr)
        _flash_step([win_p], off + half, half, (0, r), m_scr, acc_scr)
        return carry

    lax.fori_loop(c_prev, i, prev_chunk, 0)
    off = pl.multiple_of(i * KEY_CHUNK, KEY_CHUNK)
    _flash_step([sel_p, win_p], off, half, (0, r), m_scr, acc_scr)
    _flash_step([sel_p, win_p], off + half, half, (r // 2, r), m_scr, acc_scr)
    o_sel = _flash_finish(0, m_scr, acc_scr)
    o_win = _flash_finish(1, m_scr, acc_scr)

    for h in range(NSA_GROUP):
        def gate(br):
            return g_ref[0, pl.ds(br * NSA_HEADS + grp * NSA_GROUP + h, 1), :]

        o_ref[0, h * HEAD_DIM:(h + 1) * HEAD_DIM, :] = (
            head_lanes(o_cmp, h) * gate(0) + head_lanes(o_sel, h) * gate(1) + head_lanes(o_win, h) * gate(2)
        ).astype(BF16)


def _nsa_prompt(qn_t, g_t, kc, vct, kvs_t, kvw_t, ovt, expand_t, tq):
    nb, _, t = kvs_t.shape
    nq = t // tq
    nu = kc.shape[2]
    r = NSA_GROUP * tq
    kern = functools.partial(_nsa_prompt_kernel, tq=tq)

    def kmap(b, g, i):
        return (b, 0, 0)

    def vmap_(b, g, i):
        return (b, NSA_KV_HEADS + g, 0)

    def qmap(b, g, i):
        return (b, g, i)

    return pl.pallas_call(
        kern, grid=(nb, NSA_KV_HEADS, nq),
        in_specs=[pl.BlockSpec((1, NSA_GROUP * HEAD_DIM, tq), qmap),
                  pl.BlockSpec((1, GATE_ROWS, tq), lambda b, g, i: (b, 0, i)),
                  pl.BlockSpec((1, 1, nu, HEAD_DIM), lambda b, g, i: (b, g, 0, 0)),
                  pl.BlockSpec((1, 1, HEAD_DIM, nu), lambda b, g, i: (b, g, 0, 0)),
                  pl.BlockSpec((1, 2 * HEAD_DIM, t), kmap), pl.BlockSpec((1, HEAD_DIM, t), vmap_),
                  pl.BlockSpec((1, 2 * HEAD_DIM, t), kmap), pl.BlockSpec((1, HEAD_DIM, t), vmap_),
                  pl.BlockSpec(ovt.shape, lambda b, g, i: (0, 0)),
                  pl.BlockSpec(expand_t.shape, lambda b, g, i: (0, 0))],
        out_specs=pl.BlockSpec((1, NSA_GROUP * HEAD_DIM, tq), qmap),
        out_shape=jax.ShapeDtypeStruct((nb, NSA_WIDTH, t), BF16),
        scratch_shapes=[pltpu.VMEM((t, HEAD_DIM), BF16)] * 2 + [pltpu.VMEM((HEAD_DIM + BF16_ROWS, t), BF16)] * 2
        + [pltpu.VMEM((2, 8, r), F32), pltpu.VMEM((2, HEAD_DIM + BF16_ROWS, r), F32)],
        compiler_params=_cparams(("parallel", "arbitrary", "arbitrary")), name="nsa_prompt",
    )(qn_t, g_t, kc, vct, kvs_t, kvs_t, kvw_t, kvw_t, ovt, expand_t)


def _diff_lambda(lam_ref, lam_init):
    lq1, lk1, lq2, lk2 = lam_ref[0:1], lam_ref[1:2], lam_ref[2:3], lam_ref[3:4]
    return (jnp.exp(jnp.sum(lq1 * lk1, axis=-1, keepdims=True)) - jnp.exp(jnp.sum(lq2 * lk2, axis=-1, keepdims=True))
            + lam_init)


def _diff_prompt_kernel(q_ref, kt_ref, vt_ref, lam_ref, o_ref, k_sd, v_aug, m_scr, acc_scr, *, tq, lam_init):
    i = pl.program_id(2)
    t = kt_ref.shape[2]

    @pl.when(i == 0)
    def _():
        kk = kt_ref[0].T
        for j in range(4):
            k_sd[j] = kk[:, j * DIFF_QK_DIM:(j + 1) * DIFF_QK_DIM].astype(BF16)
        for h in range(2):
            v_aug[h, 0:DIFF_V_DIM] = vt_ref[0, h * DIFF_V_DIM:(h + 1) * DIFF_V_DIM, :].astype(BF16)
            v_aug[h, DIFF_V_DIM:DIFF_V_DIM + BF16_ROWS] = _ones_rows(t)

    lam = _diff_lambda(lam_ref, lam_init)
    tpos = i * tq + lax.broadcasted_iota(jnp.int32, (1, tq), 1)

    def causal(off, nk):
        kpos = off + lax.broadcasted_iota(jnp.int32, (nk, 1), 0)
        return jnp.where(kpos <= tpos, 0.0, NEG_INF)

    problems = [(j, q_ref[0, j * DIFF_QK_DIM:(j + 1) * DIFF_QK_DIM, :], k_sd.at[j], v_aug.at[j // 2], causal)
                for j in range(4)]
    _flash_init(m_scr, acc_scr)
    _flash_run(problems, 0, (i + 1) * (tq // KEY_CHUNK), m_scr, acc_scr)
    for h in range(2):
        o_ref[0, h * DIFF_V_DIM:(h + 1) * DIFF_V_DIM, :] = (
            _flash_finish(2 * h, m_scr, acc_scr) - lam * _flash_finish(2 * h + 1, m_scr, acc_scr))


def _diff_prompt(qd_t, kd_t, vd_t, lam_vec, lam_init, tq):
    nb, _, t = kd_t.shape
    nq = t // tq
    kern = functools.partial(_diff_prompt_kernel, tq=tq, lam_init=lam_init)
    return pl.pallas_call(
        kern, grid=(nb, DIFF_HEADS // 2, nq),
        in_specs=[pl.BlockSpec((1, LANES, tq), lambda b, hp, i: (b, hp, i)),
                  pl.BlockSpec((1, LANES, t), lambda b, hp, i: (b, hp, 0)),
                  pl.BlockSpec((1, LANES, t), lambda b, hp, i: (b, hp, 0)),
                  pl.BlockSpec(lam_vec.shape, lambda b, hp, i: (0, 0))],
        out_specs=pl.BlockSpec((1, LANES, tq), lambda b, hp, i: (b, hp, i)),
        out_shape=jax.ShapeDtypeStruct((nb, DIFF_WIDTH, t), F32),
        scratch_shapes=[pltpu.VMEM((4, t, DIFF_QK_DIM), BF16), pltpu.VMEM((2, DIFF_V_DIM + BF16_ROWS, t), BF16),
                        pltpu.VMEM((4, 8, tq), F32), pltpu.VMEM((4, DIFF_V_DIM + BF16_ROWS, tq), F32)],
        compiler_params=_cparams(("parallel", "arbitrary", "arbitrary")), name="diff_prompt",
    )(qd_t, kd_t, vd_t, lam_vec)


def _out_kernel(x_ref, on_ref, zn_ref, od_ref, zd_ref, dow_ref, wo_ref, y_ref, *, lam_init):
    tm = x_ref.shape[0]
    o1 = on_ref[0].astype(F32) * zn_ref[0].astype(F32)
    od = od_ref[0].reshape(DIFF_HEADS, DIFF_V_DIM, tm)
    od = od * lax.rsqrt(jnp.mean(od * od, axis=1, keepdims=True) + NORM_EPS)
    od = (od * dow_ref[...].reshape(1, DIFF_V_DIM, 1) * (1.0 - lam_init)).reshape(DIFF_WIDTH, tm)
    o2 = od * zd_ref[0].astype(F32)
    o = jnp.concatenate([o1, o2], axis=0).T.astype(BF16)
    y_ref[...] = x_ref[...] + _dot(o, wo_ref[...])


def _out_proj(x2, o_nsa_t, zn_t, od_t, zd_t, prm, lam_init, tm):
    n, d = x2.shape
    t = o_nsa_t.shape[2]
    nt = t // tm

    def tok(i):
        return (i // nt, 0, i % nt)

    def full(a):
        return pl.BlockSpec(a.shape, lambda i: (0, 0))

    consts = (prm['dow_col'], prm['w_out'])
    return pl.pallas_call(
        functools.partial(_out_kernel, lam_init=lam_init), grid=(n // tm,),
        in_specs=[pl.BlockSpec((tm, d), lambda i: (i, 0))] + [pl.BlockSpec((1, 512, tm), tok)] * 4
        + [full(a) for a in consts],
        out_specs=pl.BlockSpec((tm, d), lambda i: (i, 0)), out_shape=jax.ShapeDtypeStruct((n, d), F32),
        compiler_params=_cparams(("parallel",)), name="out_proj",
    )(x2, o_nsa_t, zn_t, od_t, zd_t, *consts)


def _mix_out(x, o_nsa, zn, od, zd, g64_ref, dow_ref, wo_ref, lam_init):
    o1 = (o_nsa * zn.astype(F32)).astype(BF16)
    odn = od * lax.rsqrt(_dot(od * od, g64_ref[...]) + NORM_EPS) * dow_ref[...] * (1.0 - lam_init)
    o2 = (odn * zd.astype(F32)).astype(BF16)
    return x + _dot(o1, wo_ref[0:NSA_WIDTH]) + _dot(o2, wo_ref[NSA_WIDTH:NSA_WIDTH + DIFF_WIDTH])


def _sample_cmp_kernel(pt_ref, q_ref, *rest, pps, past, n_sel):
    pages = rest[:pps]
    (w1_ref, posl_ref, w2_ref, g64_ref, kcw_ref, cc_ref, sac_ref, sbc_ref, ov_ref, ocmp_ref, idx_ref,
     xs_ref) = rest[pps:]
    s = pl.program_id(1)
    for k in range(pps):
        row0 = pl.multiple_of((s * pps + k) * LANES, LANES)
        for c in range(2):
            xs_ref[c, pl.ds(row0, LANES), :] = pages[k][0, c * 128:(c + 1) * 128, :].T

    @pl.when(s == pl.num_programs(1) - 1)
    def _():
        nu = past // CMP_STRIDE
        nuc = min(nu, 256)
        outs = []
        for c in range(2):
            pq = jnp.concatenate([_compress_units(xs_ref, nu, u0, nuc, w1_ref, posl_ref, w2_ref, c)
                                  for u0 in range(0, nu, nuc)], axis=0)
            outs.append(_compress_finish(pq, posl_ref, w1_ref, w2_ref, c))
        kct = _kc_norm_rope(outs[0], g64_ref, kcw_ref, cc_ref, sac_ref, sbc_ref).T.astype(BF16)
        vc = outs[1].astype(BF16)
        q = q_ref[0]
        row = lax.broadcasted_iota(jnp.int32, (NSA_HEADS, 1), 0)
        first = row < NSA_GROUP
        sc = jnp.where(first, _dot(q, kct[0:HEAD_DIM]), _dot(q, kct[HEAD_DIM:2 * HEAD_DIM]))
        cend = lax.broadcasted_iota(jnp.int32, (1, nu), 1) * CMP_STRIDE + (CMP_BLOCK - 1)
        p = _softmax_rows(sc, cend <= past)
        pb = p.astype(BF16)
        ocmp_ref[0] = jnp.where(first, _dot(pb, vc[:, 0:HEAD_DIM]), _dot(pb, vc[:, HEAD_DIM:2 * HEAD_DIM]))
        psum = jnp.concatenate([jnp.sum(p[0:NSA_GROUP], axis=0, keepdims=True),
                                jnp.sum(p[NSA_GROUP:], axis=0, keepdims=True),
                                jnp.zeros((NSA_HEADS - 2, nu), F32)], axis=0)
        hi = psum.astype(BF16)
        lo = (psum - hi.astype(F32)).astype(BF16)
        imp = _dot(hi, ov_ref[...]) + _dot(lo, ov_ref[...])
        nselp = ov_ref.shape[1]
        blk = lax.broadcasted_iota(jnp.int32, (NSA_HEADS, nselp), 1)
        cur = past // SEL_BLOCK
        forced = (blk == 0) | (blk == cur) | (blk == cur - 1)
        score = jnp.where(forced, FORCE_SCORE, jnp.where(blk * SEL_BLOCK <= past, imp, NEG_INF))
        score = jnp.where(blk < n_sel, score, EXCLUDED)
        blkf = blk.astype(F32)
        lane = lax.broadcasted_iota(jnp.int32, (NSA_HEADS, LANES), 1)
        picked = jnp.zeros((NSA_HEADS, LANES), F32)
        for k in range(min(N_SELECT, n_sel)):
            mx = jnp.max(score, axis=-1, keepdims=True)
            ik = jnp.min(jnp.where(score == mx, blkf, 1e9), axis=-1, keepdims=True)
            picked = jnp.where(lane == k, ik, picked)
            score = jnp.where(blkf == ik, EXCLUDED, score)
        idx_ref[0] = picked.astype(jnp.int32)


def _sample_cmp(page_table, q8, cache_ct, prm, cend_tables, ov, past, n_sel, pps):
    db, n_pages = page_table.shape
    n_steps = n_pages // pps
    consts = (prm['cmp_w1'], prm['cmp_posl'], prm['cmp_w2'], prm['g64_128'], prm['kcw']) + tuple(cend_tables) + (ov,)

    def full(a):
        nd = a.ndim
        return pl.BlockSpec(a.shape, lambda b, s, pt: (0,) * nd)

    def page_spec(k):
        return pl.BlockSpec((1, 256, LANES), lambda b, s, pt: (pt[b, s * pps + k], 0, 0))

    kern = functools.partial(_sample_cmp_kernel, pps=pps, past=past, n_sel=n_sel)
    return pl.pallas_call(
        kern,
        grid_spec=pltpu.PrefetchScalarGridSpec(
            num_scalar_prefetch=1, grid=(db, n_steps),
            in_specs=[pl.BlockSpec((1, NSA_HEADS, HEAD_DIM), lambda b, s, pt: (b, 0, 0))]
            + [page_spec(k) for k in range(pps)] + [full(a) for a in consts],
            out_specs=[pl.BlockSpec((1, NSA_HEADS, HEAD_DIM), lambda b, s, pt: (b, 0, 0)),
                       pl.BlockSpec((1, NSA_HEADS, LANES), lambda b, s, pt: (b, 0, 0))],
            scratch_shapes=[pltpu.VMEM((2, past, LANES), F32)]),
        out_shape=[jax.ShapeDtypeStruct((db, NSA_HEADS, HEAD_DIM), F32),
                   jax.ShapeDtypeStruct((db, NSA_HEADS, LANES), jnp.int32)],
        compiler_params=_cparams(("parallel", "arbitrary")), name="sample_cmp",
    )(page_table, q8, *([cache_ct] * pps), *consts)


def _sample_attn_kernel(pt_ref, idx_ref, q_ref, knew_ref, vnew_ref, wnew_ref, win_ref, *rest, n_k, past, n_pages):
    pages = rest[:n_k]
    osel_ref, owin_ref, wout_ref = rest[n_k:]
    b = pl.program_id(0)
    g = pl.program_id(1)
    q = q_ref[0, 0]
    lane = lax.broadcasted_iota(jnp.int32, (1, LANES), 1)
    n_cached = 2 * n_pages

    ss, vts = [], []
    for k in range(n_k):
        j = idx_ref[b, g, k]
        jc = jnp.minimum(j, n_cached - 1)
        kpos = (jc // 2) * LANES + lane
        ok = (lane // SEL_BLOCK == jc % 2) & (kpos <= past) & (j < n_cached)
        ss.append(jnp.where(ok, _dot(q, pages[k][0, 0, 0].astype(BF16)), NEG_INF))
        vts.append(pages[k][0, 1, 0].astype(BF16))
    s_new = jnp.sum(q.astype(F32) * knew_ref[0, 0][0:1], axis=-1, keepdims=True)
    m = s_new
    for s in ss:
        m = jnp.maximum(m, jnp.max(s, axis=-1, keepdims=True))
    e_new = jnp.exp(s_new - m)
    l = e_new
    acc = e_new * vnew_ref[0, 0][0:1]
    for s, vt in zip(ss, vts):
        e = jnp.exp(s - m)
        l = l + jnp.sum(e, axis=-1, keepdims=True)
        acc = acc + _dot_nt(e.astype(BF16), vt)
    osel_ref[0, 0] = acc / jnp.maximum(l, 1e-30)

    w = win_ref.shape[4]
    wl = lax.broadcasted_iota(jnp.int32, (1, w), 1)
    kt = jnp.where(wl == w - 1, wnew_ref[0, 0, 0], pltpu.roll(win_ref[0, 0, 0], w - 1, 1))
    vt = jnp.where(wl == w - 1, wnew_ref[0, 1, 0], pltpu.roll(win_ref[0, 1, 0], w - 1, 1))
    wout_ref[0, 0, 0] = kt
    wout_ref[0, 1, 0] = vt
    kpos = past - (w - 1) + wl
    d = past - kpos
    p = _softmax_rows(_dot(q, kt.astype(BF16)), (d >= 0) & (d < WINDOW) & (kpos >= 0))
    owin_ref[0, 0] = _dot_nt(p.astype(BF16), vt.astype(BF16))


def _sample_attn(page_table, idx, q4, knew, vnew, wnew, win_t, cache_st, past):
    db, n_pages = page_table.shape
    n_k = idx.shape[2]
    w = win_t.shape[4]

    def per_bg(shape):
        nd = len(shape)
        return pl.BlockSpec((1, 1) + shape, lambda b, g, pt, ix: (b, g) + (0,) * nd)

    def kv_bg(last):
        return pl.BlockSpec((1, 2, 1, HEAD_DIM, last), lambda b, g, pt, ix: (b, 0, g, 0, 0))

    def page_spec(k):
        def imap(b, g, pt, ix):
            j = jnp.minimum(ix[b, g, k], 2 * n_pages - 1)
            return (pt[b, j // 2], 0, g, 0, 0)
        return pl.BlockSpec((1, 2, 1, HEAD_DIM, LANES), imap)

    kern = functools.partial(_sample_attn_kernel, n_k=n_k, past=past, n_pages=n_pages)
    o_shape = jax.ShapeDtypeStruct((db, NSA_KV_HEADS, 8, HEAD_DIM), F32)
    return pl.pallas_call(
        kern,
        grid_spec=pltpu.PrefetchScalarGridSpec(
            num_scalar_prefetch=2, grid=(db, NSA_KV_HEADS),
            in_specs=[per_bg((8, HEAD_DIM))] * 3 + [kv_bg(1), kv_bg(w)] + [page_spec(k) for k in range(n_k)],
            out_specs=[per_bg((8, HEAD_DIM)), per_bg((8, HEAD_DIM)), kv_bg(w)]),
        out_shape=[o_shape, o_shape, jax.ShapeDtypeStruct(win_t.shape, F32)],
        compiler_params=_cparams(("parallel", "arbitrary")), name="sample_attn",
    )(page_table, idx, q4, knew, vnew, wnew, win_t, *([cache_st] * n_k))


def _sample_diff_kernel(pt_ref, qbd_ref, q16_ref, k16_ref, vnew_ref, lam_ref, *rest, pps, lam_init):
    kpages = rest[:pps]
    vpages = rest[pps:2 * pps]
    o_ref, m_scr, l_scr, acc_scr = rest[2 * pps:]
    s = pl.program_id(1)
    rows = 2 * DIFF_HEADS

    @pl.when(s == 0)
    def _():
        m_scr[...] = jnp.full(m_scr.shape, NEG_INF, F32)
        l_scr[...] = jnp.zeros(l_scr.shape, F32)
        acc_scr[...] = jnp.zeros(acc_scr.shape, F32)

    qbd = qbd_ref[0]
    ss = [_dot(qbd, kp[0].astype(BF16)) for kp in kpages]
    cm = ss[0]
    for sk in ss[1:]:
        cm = jnp.maximum(cm, sk)
    m_old = m_scr[...]
    m_new = jnp.maximum(m_old, jnp.max(cm, axis=-1, keepdims=True))
    alpha = jnp.exp(m_old - m_new)
    lsum = jnp.zeros((rows, LANES), F32)
    accs = jnp.zeros(acc_scr.shape, F32)
    for sk, vp in zip(ss, vpages):
        e = jnp.exp(sk - m_new)
        lsum = lsum + e
        accs = accs + _dot_nt(e.astype(BF16), vp[0].astype(BF16))
    m_scr[...] = m_new
    l_scr[...] = alpha * l_scr[...] + jnp.sum(lsum, axis=-1, keepdims=True)
    acc_scr[...] = alpha[:, 0:1] * acc_scr[...] + accs

    @pl.when(s == pl.num_programs(1) - 1)
    def _():
        s_new = jnp.sum(q16_ref[0] * k16_ref[0], axis=-1, keepdims=True)
        m1 = m_scr[...][:, 0:1]
        m2 = jnp.maximum(m1, s_new)
        a2 = jnp.exp(m1 - m2)
        e_new = jnp.exp(s_new - m2)
        l = a2 * l_scr[...][:, 0:1] + e_new
        acc = acc_scr[...]
        col = lax.broadcasted_iota(jnp.int32, acc.shape, 1) // DIFF_V_DIM
        rowh = lax.broadcasted_iota(jnp.int32, acc.shape, 0) % DIFF_HEADS
        acc = jnp.where(col == rowh, acc, 0.0)
        o16 = acc[:, 0:DIFF_V_DIM]
        for h in range(1, DIFF_HEADS):
            o16 = o16 + acc[:, h * DIFF_V_DIM:(h + 1) * DIFF_V_DIM]
        vnew = vnew_ref[0]
        o16 = (a2 * o16 + e_new * jnp.concatenate([vnew, vnew], axis=0)) / jnp.maximum(l, 1e-30)
        o_ref[0] = o16[0:DIFF_HEADS] - _diff_lambda(lam_ref, lam_init) * o16[DIFF_HEADS:]


def _sample_diff(page_table, qbd, q16, k16, vnew, lam_vec, cache_dkt, cache_dvt, lam_init, pps):
    db, n_pages = page_table.shape
    n_steps = n_pages // pps
    rows = 2 * DIFF_HEADS

    def per_b(shape):
        nd = len(shape)
        return pl.BlockSpec((1,) + shape, lambda b, s, pt: (b,) + (0,) * nd)

    def page_spec(k):
        return pl.BlockSpec((1, 512, LANES), lambda b, s, pt: (pt[b, s * pps + k], 0, 0))

    kern = functools.partial(_sample_diff_kernel, pps=pps, lam_init=lam_init)
    return pl.pallas_call(
        kern,
        grid_spec=pltpu.PrefetchScalarGridSpec(
            num_scalar_prefetch=1, grid=(db, n_steps),
            in_specs=[per_b((rows, 512)), per_b((rows, DIFF_QK_DIM)), per_b((rows, DIFF_QK_DIM)),
                      per_b((DIFF_HEADS, DIFF_V_DIM)), pl.BlockSpec(lam_vec.shape, lambda b, s, pt: (0, 0))]
            + [page_spec(k) for k in range(pps)] * 2,
            out_specs=per_b((DIFF_HEADS, DIFF_V_DIM)),
            scratch_shapes=[pltpu.VMEM((rows, LANES), F32), pltpu.VMEM((rows, LANES), F32),
                            pltpu.VMEM((rows, 512), F32)]),
        out_shape=jax.ShapeDtypeStruct((db, DIFF_HEADS, DIFF_V_DIM), F32),
        compiler_params=_cparams(("parallel", "arbitrary")), name="sample_diff",
    )(page_table, qbd, q16, k16, vnew, lam_vec, *([cache_dkt] * pps), *([cache_dvt] * pps))


def _sample_out_kernel(x_ref, oc_ref, os_ref, ow_ref, g0_ref, g1_ref, g2_ref, zn_ref, od_ref, zd_ref, g64_ref, dow_ref,
                       wo_ref, y_ref, *, lam_init):
    o_nsa = oc_ref[...] * g0_ref[...] + os_ref[...] * g1_ref[...] + ow_ref[...] * g2_ref[...]
    y_ref[...] = _mix_out(x_ref[...], o_nsa, zn_ref[...], od_ref[...], zd_ref[...], g64_ref, dow_ref, wo_ref, lam_init)


def _sample_out(x2, branches, gates, zn, od, zd, prm, lam_init):
    args = (x2,) + tuple(branches) + tuple(gates) + (zn, od, zd, prm['g64'], prm['dow'], prm['w_out'])
    return pl.pallas_call(
        functools.partial(_sample_out_kernel, lam_init=lam_init),
        out_shape=jax.ShapeDtypeStruct(x2.shape, F32), compiler_params=_cparams(None), name="sample_out",
    )(*args)


def _layer_params(lp):
    w = lp['w_in']
    d = w.shape[0]
    cols = [w[:, _OFFS[k]:_OFFS[k + 1]] for k in _ROW_ORDER]
    cols.append(jnp.zeros((d, GATE_ROWS - _SIZES[4]), F32))
    w_t = jnp.concatenate(cols, axis=1).T

    def cmp_w1(w1):
        w1 = w1.reshape(2, CMP_STRIDE, HEAD_DIM, HEAD_DIM)
        eye = jnp.eye(NSA_KV_HEADS, dtype=F32)
        full = jnp.einsum('hlde,gk->lgdhke', w1, eye)
        return full.reshape(CMP_STRIDE * NSA_KV_HEADS * HEAD_DIM, 2 * NSA_KV_HEADS * HEAD_DIM)

    def cmp_pos(pos):
        pos = pos.reshape(2, CMP_STRIDE, 1, HEAD_DIM)
        rows = jnp.broadcast_to(pos, (2, CMP_STRIDE, NSA_KV_HEADS, HEAD_DIM)).reshape(2, -1)
        return jnp.concatenate([rows, jnp.zeros((6, rows.shape[1]), F32)], axis=0)

    def cmp_w2(w2):
        return jnp.kron(jnp.eye(NSA_KV_HEADS, dtype=F32), w2)

    lam_vec = jnp.concatenate([lp['lambda_q1'][None], lp['lambda_k1'][None], lp['lambda_q2'][None],
                               lp['lambda_k2'][None], jnp.zeros((4, DIFF_QK_DIM), F32)], axis=0)
    return {
        'norm_w': lp['norm_w'][None], 'w_t': w_t.astype(BF16),
        'g64': _group_mean_matrix(512, HEAD_DIM), 'g64_128': _group_mean_matrix(LANES, HEAD_DIM),
        'qnw': lp['nsa_q_norm'][:, None], 'qdw': lp['diff_q_norm'][:, None],
        'ksw': lp['nsa_ks_norm'][:, None], 'kww': lp['nsa_kw_norm'][:, None], 'kdw': lp['diff_k_norm'][:, None],
        'kcw': jnp.tile(lp['nsa_kc_norm'], NSA_KV_HEADS)[None],
        'cmp_w1': jnp.stack([cmp_w1(lp['cmp_w1_k']), cmp_w1(lp['cmp_w1_v'])]).astype(BF16),
        'cmp_posl': jnp.stack([cmp_pos(lp['cmp_pos_k']), cmp_pos(lp['cmp_pos_v'])]).astype(BF16),
        'cmp_w2': jnp.stack([cmp_w2(lp['cmp_w2_k']), cmp_w2(lp['cmp_w2_v'])]).astype(BF16),
        'dow': jnp.tile(lp['diff_out_norm'], DIFF_HEADS)[None], 'dow_col': lp['diff_out_norm'][:, None],
        'w_out': lp['w_out'].astype(BF16), 'lam_vec': lam_vec,
    }


def _overlap_t(n_sel_pad, nu, n_cmp, n_sel):
    cs = np.arange(nu)[None, :] * CMP_STRIDE
    ss = np.arange(n_sel_pad)[:, None] * SEL_BLOCK
    ov = np.clip(np.minimum(cs + CMP_BLOCK, ss + SEL_BLOCK) - np.maximum(cs, ss), 0, None) / CMP_STRIDE
    ov = ov * (np.arange(nu)[None, :] < n_cmp) * (np.arange(n_sel_pad)[:, None] < n_sel)
    return jnp.asarray(ov, dtype=BF16)


def _prompt_layer(x, prm, lam_init):
    b, t, d = x.shape
    x2 = x.reshape(b * t, d)
    tables = _rope_tables(np.arange(t))
    qn_t, kvc_t, kvs_t, kvw_t, zn_t, qd_t, kd_t, vd_t, zd_t, g_t = _project(x2, tables, t, prm, 512, True)
    nu = t // CMP_STRIDE
    n_cmp = nu - CMP_BLOCK // CMP_STRIDE + 1
    cend_tables = _rope_tables(np.arange(nu) * CMP_STRIDE + CMP_BLOCK - 1)[0]
    kc, vct = _compress_prompt(kvc_t, prm, cend_tables)
    n_sel = -(-t // SEL_BLOCK)
    ovt = _overlap_t(n_sel, nu, n_cmp, n_sel)
    expand_t = jnp.asarray(np.arange(t)[:, None] // SEL_BLOCK == np.arange(n_sel)[None, :], dtype=BF16)
    o_nsa_t = _nsa_prompt(qn_t, g_t, kc, vct, kvs_t, kvw_t, ovt, expand_t, 512)
    o_d_t = _diff_prompt(qd_t, kd_t, vd_t, prm['lam_vec'], lam_init, 512)
    y = _out_proj(x2, o_nsa_t, zn_t, o_d_t, zd_t, prm, lam_init, 512).reshape(b, t, d)

    def state(a_t, shape_tail):
        return jnp.transpose(a_t.reshape((b,) + shape_tail + (a_t.shape[-1],)),
                             (0, len(shape_tail) + 1) + tuple(range(1, len(shape_tail) + 1)))

    kv_tail = (2, NSA_KV_HEADS, HEAD_DIM)
    w = min(WINDOW, t)
    return y, (state(kvc_t, kv_tail), state(kvs_t, kv_tail), state(kvw_t[:, :, t - w:], kv_tail),
               state(kd_t, (DIFF_HEADS, 2, DIFF_QK_DIM)), state(vd_t, (DIFF_HEADS, DIFF_V_DIM)))


def _sample_layer(x, cache_c, cache_s, cache_dk, cache_dv, win, page_table, prm, lam_init):
    db, t, d = x.shape
    assert t == 1 and db % 8 == 0
    n_pool, page = cache_c.shape[:2]
    assert page == LANES
    n_pages = page_table.shape[1]
    past = n_pages * page
    x2 = x.reshape(db, d)
    dbp = -(-db // LANES) * LANES
    tables = _rope_tables(np.full((dbp,), past))
    proj = _project(jnp.pad(x2, ((0, dbp - db), (0, 0))), tables, dbp, prm, dbp, False)
    qn, kvc, kvs, kvw, zn, qd, kd, vd, zd, gn = [a[0, :, :db].T for a in proj]

    cache_ct = jnp.transpose(cache_c, (0, 2, 3, 4, 1)).reshape(n_pool, 256, page)
    cache_st = jnp.transpose(cache_s, (0, 2, 3, 4, 1))
    cache_dkt = jnp.transpose(cache_dk, (0, 2, 3, 4, 1)).reshape(n_pool, 512, page)
    cache_dvt = jnp.transpose(cache_dv, (0, 2, 3, 1)).reshape(n_pool, 512, page)
    win_t = jnp.transpose(win, (0, 2, 3, 4, 1))

    nu = past // CMP_STRIDE
    n_cmp = nu - CMP_BLOCK // CMP_STRIDE + 1
    n_sel = -(-(past + t) // SEL_BLOCK)
    n_sel_pad = -(-n_sel // LANES) * LANES
    ov = _overlap_t(n_sel_pad, nu, n_cmp, n_sel).T
    cend_tables = _rope_tables(np.arange(nu) * CMP_STRIDE + CMP_BLOCK - 1)[0]
    pps = min(32, n_pages)
    o_cmp, picked = _sample_cmp(page_table, qn.reshape(db, NSA_HEADS, HEAD_DIM), cache_ct, prm, cend_tables, ov, past,
                                n_sel, pps)
    n_k = min(N_SELECT, n_sel)
    idx = picked[:, :NSA_KV_HEADS, :n_k]

    def rows8(a):
        return jnp.pad(a, ((0, 0), (0, 0), (0, 8 - a.shape[2]), (0, 0)))

    kvs5 = kvs.reshape(db, 2, NSA_KV_HEADS, 1, HEAD_DIM)
    q4 = rows8(qn.reshape(db, NSA_KV_HEADS, NSA_GROUP, HEAD_DIM))
    o_sel, o_win, win_new = _sample_attn(page_table, idx, q4, rows8(kvs5[:, 0]), rows8(kvs5[:, 1]),
                                         kvw.reshape(db, 2, NSA_KV_HEADS, HEAD_DIM, 1), win_t, cache_st, past)

    qd3 = qd.astype(F32).reshape(db, DIFF_HEADS, 2, DIFF_QK_DIM)
    q16 = jnp.transpose(qd3, (0, 2, 1, 3)).reshape(db, 2 * DIFF_HEADS, DIFF_QK_DIM)
    own = np.zeros((2 * DIFF_HEADS, DIFF_QK_COLS), np.float32)
    for c in range(2):
        for h in range(DIFF_HEADS):
            lo = h * 2 * DIFF_QK_DIM + c * DIFF_QK_DIM
            own[c * DIFF_HEADS + h, lo:lo + DIFF_QK_DIM] = 1.0
    qbd = (qd.astype(F32)[:, None, :] * own[None]).astype(BF16)
    kd3 = kd.reshape(db, DIFF_HEADS, 2, DIFF_QK_DIM)
    k16 = jnp.transpose(kd3, (0, 2, 1, 3)).reshape(db, 2 * DIFF_HEADS, DIFF_QK_DIM)
    vnew = vd.reshape(db, DIFF_HEADS, DIFF_V_DIM)
    o_d = _sample_diff(page_table, qbd, q16, k16, vnew, prm['lam_vec'], cache_dkt, cache_dvt, lam_init,
                       min(32, n_pages))

    gates = [jnp.repeat(gn[:, br * NSA_HEADS:(br + 1) * NSA_HEADS], HEAD_DIM, axis=1) for br in range(3)]
    branches = (o_cmp.reshape(db, NSA_WIDTH), o_sel[:, :, :NSA_GROUP].reshape(db, NSA_WIDTH),
                o_win[:, :, :NSA_GROUP].reshape(db, NSA_WIDTH))
    y = _sample_out(x2, branches, gates, zn, o_d.reshape(db, DIFF_WIDTH), zd, prm, lam_init).reshape(db, t, d)

    kv_shape = (db, t, 2, NSA_KV_HEADS, HEAD_DIM)
    states = (kvc.reshape(kv_shape), kvs.reshape(kv_shape), jnp.transpose(win_new, (0, 4, 1, 2, 3)),
              kd3.reshape(db, t, DIFF_HEADS, 2, DIFF_QK_DIM), vnew.reshape(db, t, DIFF_HEADS, DIFF_V_DIM))
    return y, states


def kernel(x_prompt, x_sample, cache_nsa_cmp_kv, cache_nsa_sel_kv, cache_diff_k, cache_diff_v, state_nsa_win_kv, page_table, norm_w, w_in, nsa_q_norm, nsa_kc_norm, nsa_ks_norm, nsa_kw_norm, cmp_pos_k, cmp_w1_k, cmp_w2_k, cmp_pos_v, cmp_w1_v, cmp_w2_v, diff_q_norm, diff_k_norm, lambda_q1, lambda_k1, lambda_q2, lambda_k2, diff_out_norm, w_out):
    depth = w_in.shape[0]
    y_p, y_s = x_prompt, x_sample
    p_states, s_states = [], []
    for layer in range(depth):
        lp = {
            'norm_w': norm_w[layer], 'w_in': w_in[layer],
            'nsa_q_norm': nsa_q_norm[layer], 'nsa_kc_norm': nsa_kc_norm[layer],
            'nsa_ks_norm': nsa_ks_norm[layer], 'nsa_kw_norm': nsa_kw_norm[layer],
            'cmp_pos_k': cmp_pos_k[layer], 'cmp_w1_k': cmp_w1_k[layer], 'cmp_w2_k': cmp_w2_k[layer],
            'cmp_pos_v': cmp_pos_v[layer], 'cmp_w1_v': cmp_w1_v[layer], 'cmp_w2_v': cmp_w2_v[layer],
            'diff_q_norm': diff_q_norm[layer], 'diff_k_norm': diff_k_norm[layer],
            'lambda_q1': lambda_q1[layer], 'lambda_k1': lambda_k1[layer],
            'lambda_q2': lambda_q2[layer], 'lambda_k2': lambda_k2[layer],
            'diff_out_norm': diff_out_norm[layer], 'w_out': w_out[layer],
        }
        lam_init = 0.8 - 0.6 * math.exp(-0.3 * layer)
        prm = _layer_params(lp)
        y_p, ps = _prompt_layer(y_p, prm, lam_init)
        y_s, ss = _sample_layer(y_s, cache_nsa_cmp_kv[layer], cache_nsa_sel_kv[layer], cache_diff_k[layer],
                                cache_diff_v[layer], state_nsa_win_kv[layer], page_table, prm, lam_init)
        p_states.append(ps)
        s_states.append(ss)
    p_c, p_s, p_w, p_dk, p_dv = [jnp.stack(t, axis=0) for t in zip(*p_states)]
    s_c, s_s, s_w, s_dk, s_dv = [jnp.stack(t, axis=0) for t in zip(*s_states)]
    return (y_p, y_s, p_c, p_s, p_w, p_dk, p_dv, s_c, s_s, s_w, s_dk, s_dv)
```

```python
import functools
import math

import jax
import jax.numpy as jnp
import numpy as np
from jax import lax
from jax.experimental import pallas as pl
from jax.experimental.pallas import tpu as pltpu

HEAD_DIM = 64
NSA_HEADS = 8
NSA_KV_HEADS = 2
NSA_GROUP = NSA_HEADS // NSA_KV_HEADS
NSA_WIDTH = NSA_HEADS * HEAD_DIM
CMP_BLOCK = 32
CMP_STRIDE = 16
SEL_BLOCK = 64
N_SELECT = 16
WINDOW = 512
DIFF_HEADS = 8
DIFF_QK_DIM = 32
DIFF_V_DIM = 64
DIFF_WIDTH = DIFF_HEADS * DIFF_V_DIM
KV_COLS = 2 * NSA_KV_HEADS * HEAD_DIM
DIFF_QK_COLS = DIFF_HEADS * 2 * DIFF_QK_DIM
ROPE_THETA = 500000.0
ROT_FRACTION = 4
NORM_EPS = 1e-6
NEG_INF = -1e30
FORCE_SCORE = 1e9
EXCLUDED = -3e38
LOG2E = 1.4426950408889634

LANES = 128
BF16_ROWS = 16
KEY_CHUNK = 512
VMEM_LIMIT = 56 * 1024 * 1024

F32 = jnp.float32
BF16 = jnp.bfloat16

_SIZES = (NSA_WIDTH, KV_COLS, KV_COLS, KV_COLS, 3 * NSA_HEADS, NSA_WIDTH, DIFF_QK_COLS, DIFF_QK_COLS, DIFF_WIDTH,
          DIFF_WIDTH)
_OFFS = tuple(int(v) for v in np.concatenate([[0], np.cumsum(_SIZES)]))
_ROW_ORDER = (0, 1, 2, 3, 5, 6, 7, 8, 9, 4)
GATE_ROWS = 32
_ROW_SIZES = tuple(_SIZES[k] for k in _ROW_ORDER[:-1]) + (GATE_ROWS,)
_ROW_OFFS = tuple(int(v) for v in np.concatenate([[0], np.cumsum(_ROW_SIZES)]))
PROJ_ROWS = _ROW_OFFS[-1]


def _cparams(sem):
    return pltpu.CompilerParams(dimension_semantics=sem, vmem_limit_bytes=VMEM_LIMIT)


def _sigmoid(x):
    return 1.0 / (1.0 + jnp.exp(-x))


def _dot(a, b):
    return jnp.dot(a, b, preferred_element_type=F32)


def _dot_nt(a, b):
    return lax.dot_general(a, b, (((1,), (1,)), ((), ())), preferred_element_type=F32)


def _rope_lanes(x, c, sa, sb, half):
    parts = []
    for k in range(x.shape[1] // LANES):
        xk = x[:, LANES * k:LANES * (k + 1)]
        parts.append(xk * c + pltpu.roll(xk, LANES - half, 1) * sa + pltpu.roll(xk, half, 1) * sb)
    return jnp.concatenate(parts, axis=1) if len(parts) > 1 else parts[0]


def _norm_rope_rows(x, w_ref, c, s, groups, gd):
    tm = x.shape[1]
    x3 = x.reshape(groups, gd, tm)
    r = lax.rsqrt(jnp.mean(x3 * x3, axis=1, keepdims=True) + NORM_EPS)
    x3 = (x3 * r) * w_ref[...].reshape(1, gd, 1)
    if gd == HEAD_DIM:
        x1 = x3[:, 0:8]
        x2 = x3[:, 8:16]
        parts = [x1 * c - x2 * s, x2 * c + x1 * s, x3[:, 16:]]
    else:
        rot = x3[:, 0:8].reshape(groups * 8, tm)
        first = lax.broadcasted_iota(jnp.int32, rot.shape, 0) % 8 < 4
        swapped = jnp.where(first, pltpu.roll(rot, rot.shape[0] - 4, 0), pltpu.roll(rot, 4, 0))
        parts = [rot.reshape(groups, 8, tm) * c + swapped.reshape(groups, 8, tm) * s, x3[:, 8:]]
    return jnp.concatenate(parts, axis=1).reshape(groups * gd, tm)


def _proj_kernel(x_ref, nw_ref, wt_ref, qnw_ref, ksw_ref, kww_ref, qdw_ref, kdw_ref, c8_ref, s8_ref, c4_ref, s4_ref,
                 qn_ref, kvc_ref, kvs_ref, kvw_ref, zn_ref, qd_ref, kd_ref, vd_ref, zd_ref, g_ref, *, qn_scale,
                 qd_scale):
    x = x_ref[...]
    ms = jnp.mean(x * x, axis=-1, keepdims=True)
    ht = ((x * lax.rsqrt(ms + NORM_EPS)) * nw_ref[...]).T.astype(BF16)

    def rows(k):
        return _dot(wt_ref[_ROW_OFFS[k]:_ROW_OFFS[k + 1]], ht)

    c8, s8, c4, s4 = c8_ref[...], s8_ref[...], c4_ref[...], s4_ref[...]
    qn_ref[0] = (_norm_rope_rows(rows(0), qnw_ref, c8, s8, NSA_HEADS, HEAD_DIM) * qn_scale).astype(BF16)
    kvc_ref[0] = rows(1)
    kvs = rows(2)
    kvs_ref[0, 0:128] = _norm_rope_rows(kvs[0:128], ksw_ref, c8, s8, NSA_KV_HEADS, HEAD_DIM)
    kvs_ref[0, 128:256] = kvs[128:256]
    kvw = rows(3)
    kvw_ref[0, 0:128] = _norm_rope_rows(kvw[0:128], kww_ref, c8, s8, NSA_KV_HEADS, HEAD_DIM)
    kvw_ref[0, 128:256] = kvw[128:256]
    zn = rows(4)
    zn_ref[0] = (zn * _sigmoid(zn)).astype(BF16)
    qd_ref[0] = (_norm_rope_rows(rows(5), qdw_ref, c4, s4, 2 * DIFF_HEADS, DIFF_QK_DIM) * qd_scale).astype(BF16)
    kd_ref[0] = _norm_rope_rows(rows(6), kdw_ref, c4, s4, 2 * DIFF_HEADS, DIFF_QK_DIM)
    vd_ref[0] = rows(7)
    zd = rows(8)
    zd_ref[0] = (zd * _sigmoid(zd)).astype(BF16)
    g_ref[0] = _sigmoid(rows(9))


def _rope_tables(pos):
    pos = np.asarray(pos, np.float32)
    lane = np.arange(LANES)

    def tables(group, half):
        inv = np.float32(ROPE_THETA) ** (-np.arange(half, dtype=np.float32) / np.float32(half))
        ang = (pos[:, None] * inv).astype(np.float32)
        cos, sin = np.cos(ang), np.sin(ang)
        m = lane % group
        cosl, sinl = cos[:, m % half], sin[:, m % half]
        c = np.where(m < 2 * half, cosl, 1.0)
        sa = np.where(m < half, -sinl, 0.0)
        sb = np.where((m >= half) & (m < 2 * half), sinl, 0.0)
        return tuple(jnp.asarray(a, F32) for a in (c, sa, sb)), (cos.T, sin.T)

    tq, (c8, s8) = tables(HEAD_DIM, HEAD_DIM // ROT_FRACTION // 2)
    td, (c4, s4) = tables(DIFF_QK_DIM, DIFF_QK_DIM // ROT_FRACTION // 2)
    c4 = np.concatenate([c4, c4], axis=0)
    s4 = np.concatenate([-s4, s4], axis=0)
    return tq, td, tuple(jnp.asarray(a, F32) for a in (c8, s8)), tuple(jnp.asarray(a, F32) for a in (c4, s4))


def _group_mean_matrix(width, group):
    idx = np.arange(width)
    return jnp.asarray((idx[:, None] // group == idx[None, :] // group) / group, dtype=F32)


def _project(x2, pos_tables, t_per_batch, prm, tm, log2e_in_q):
    n, d = x2.shape
    nb = n // t_per_batch
    nt = t_per_batch // tm
    (c8, s8), (c4, s4) = pos_tables[2], pos_tables[3]
    fold = LOG2E if log2e_in_q else 1.0

    def full(a):
        return pl.BlockSpec(a.shape, lambda i: (0, 0))

    def out(i):
        return (i // nt, 0, i % nt)

    consts = (prm['norm_w'], prm['w_t'], prm['qnw'], prm['ksw'], prm['kww'], prm['qdw'], prm['kdw'])
    dts = (BF16, F32, F32, F32, BF16, BF16, F32, F32, BF16, F32)
    kern = functools.partial(_proj_kernel, qn_scale=HEAD_DIM ** -0.5 * fold, qd_scale=DIFF_QK_DIM ** -0.5 * fold)
    return pl.pallas_call(
        kern, grid=(n // tm,),
        in_specs=[pl.BlockSpec((tm, d), lambda i: (i, 0))] + [full(a) for a in consts]
        + [pl.BlockSpec((8, tm), lambda i: (0, i % nt))] * 4,
        out_specs=[pl.BlockSpec((1, r, tm), out) for r in _ROW_SIZES],
        out_shape=[jax.ShapeDtypeStruct((nb, r, t_per_batch), dt) for r, dt in zip(_ROW_SIZES, dts)],
        compiler_params=_cparams(("parallel",)), name="proj",
    )(x2, *consts, c8, s8, c4, s4)


def _compress_units(xs_ref, nu, u0, nuc, w1_ref, posl_ref, w2_ref, c):
    lhs = jnp.concatenate(
        [xs_ref[c, pl.ds(u0 * CMP_STRIDE + l, nuc, stride=CMP_STRIDE), :] for l in range(CMP_STRIDE)],
        axis=1).astype(BF16)
    return _dot(lhs, w1_ref[c])


def _compress_finish(pq, posl_ref, w1_ref, w2_ref, c):
    nu = pq.shape[0]
    pb = _dot(posl_ref[c], w1_ref[c])
    bias = pb[0:1, 0:128] + pb[1:2, 128:256]
    hid = pq[:, 0:128] + pltpu.roll(pq[:, 128:256], nu - 1, 0) + bias
    hid = hid * _sigmoid(hid)
    return _dot(hid.astype(BF16), w2_ref[c])


def _kc_norm_rope(kc, g64_ref, kcw_ref, cc_ref, sac_ref, sbc_ref):
    kc = kc * lax.rsqrt(_dot(kc * kc, g64_ref[...]) + NORM_EPS) * kcw_ref[...]
    return _rope_lanes(kc, cc_ref[...], sac_ref[...], sbc_ref[...], 8)


def _cmp_prompt_kernel(kvc_ref, w1_ref, posl_ref, w2_ref, g64_ref, kcw_ref, cc_ref, sac_ref, sbc_ref,
                       kc_ref, vct_ref, xs_ref):
    t = kvc_ref.shape[2]
    nu = t // CMP_STRIDE
    for c in range(2):
        xs_ref[c] = kvc_ref[0, c * 128:(c + 1) * 128, :].T
    outs = []
    for c in range(2):
        pq = _compress_units(xs_ref, nu, 0, nu, w1_ref, posl_ref, w2_ref, c)
        outs.append(_compress_finish(pq, posl_ref, w1_ref, w2_ref, c))
    kc = _kc_norm_rope(outs[0], g64_ref, kcw_ref, cc_ref, sac_ref, sbc_ref)
    vct = outs[1].T
    for g in range(NSA_KV_HEADS):
        kc_ref[0, g] = kc[:, g * HEAD_DIM:(g + 1) * HEAD_DIM]
        vct_ref[0, g] = vct[g * HEAD_DIM:(g + 1) * HEAD_DIM]


def _compress_prompt(kvc_t, prm, cend_tables):
    nb, _, t = kvc_t.shape
    nu = t // CMP_STRIDE
    cc, sac, sbc = cend_tables

    def full(a):
        nd = a.ndim
        return pl.BlockSpec(a.shape, lambda b: (0,) * nd)

    consts = (prm['cmp_w1'], prm['cmp_posl'], prm['cmp_w2'], prm['g64_128'], prm['kcw'], cc, sac, sbc)
    return pl.pallas_call(
        _cmp_prompt_kernel, grid=(nb,),
        in_specs=[pl.BlockSpec((1, 256, t), lambda b: (b, 0, 0))] + [full(a) for a in consts],
        out_specs=[pl.BlockSpec((1, NSA_KV_HEADS, nu, HEAD_DIM), lambda b: (b, 0, 0, 0)),
                   pl.BlockSpec((1, NSA_KV_HEADS, HEAD_DIM, nu), lambda b: (b, 0, 0, 0))],
        out_shape=[jax.ShapeDtypeStruct((nb, NSA_KV_HEADS, nu, HEAD_DIM), F32),
                   jax.ShapeDtypeStruct((nb, NSA_KV_HEADS, HEAD_DIM, nu), F32)],
        scratch_shapes=[pltpu.VMEM((2, t, LANES), F32)],
        compiler_params=_cparams(("parallel",)), name="cmp_prompt",
    )(kvc_t, *consts)


def _softmax_rows(s, mask):
    sm = jnp.where(mask, s, NEG_INF)
    m = jnp.max(sm, axis=-1, keepdims=True)
    e = jnp.where(mask, jnp.exp(sm - m), 0.0)
    return e / jnp.maximum(jnp.sum(e, axis=-1, keepdims=True), 1e-30)


def _ones_rows(t):
    return jnp.where(lax.broadcasted_iota(jnp.int32, (BF16_ROWS, t), 0) == 0, 1.0, 0.0).astype(BF16)


def _flash_init(m_scr, acc_scr):
    m_scr[...] = jnp.full(m_scr.shape, NEG_INF, F32)
    acc_scr[...] = jnp.zeros(acc_scr.shape, F32)


def _flash_step(problems, off, nk, lanes, m_scr, acc_scr):
    lo, hi = lanes
    for n, qt, k_ref, v_ref, bias_fn in problems:
        s = _dot(k_ref[pl.ds(off, nk), :], qt[:, lo:hi]) + bias_fn(off, nk)[:, lo:hi]
        m_old = m_scr[n, 0:1, lo:hi]
        m_new = jnp.maximum(m_old, jnp.max(s, axis=0, keepdims=True))
        p = jnp.exp2(s - m_new).astype(BF16)
        acc_scr[n, :, lo:hi] = jnp.exp2(m_old - m_new) * acc_scr[n, :, lo:hi] + _dot(v_ref[:, pl.ds(off, nk)], p)
        m_scr[n, :, lo:hi] = jnp.broadcast_to(m_new, (m_scr.shape[1], hi - lo))


def _flash_run(problems, c_lo, c_hi, m_scr, acc_scr):
    r = m_scr.shape[2]

    def body(c, carry):
        _flash_step(problems, pl.multiple_of(c * KEY_CHUNK, KEY_CHUNK), KEY_CHUNK, (0, r), m_scr, acc_scr)
        return carry

    lax.fori_loop(c_lo, c_hi, body, 0)


def _flash_finish(n, m_scr, acc_scr):
    dv = acc_scr.shape[1] - BF16_ROWS
    acc = acc_scr[n]
    return jnp.where(m_scr[n, 0:1] > 0.5 * NEG_INF, acc[0:dv] / jnp.maximum(acc[dv:dv + 1], 1e-30), 0.0)


def _nsa_prompt_kernel(q_ref, g_ref, kc_ref, vct_ref, kst_ref, vst_ref, kwt_ref, vwt_ref, ovt_ref, exp_ref, o_ref,
                       ks_sd, kw_sd, vs_aug, vw_aug, m_scr, acc_scr, *, tq):
    grp = pl.program_id(1)
    i = pl.program_id(2)
    t = vst_ref.shape[2]

    @pl.when(i == 0)
    def _():
        def group_cols(kt_ref):
            kk = kt_ref[0].T
            return jnp.where(grp == 0, kk[:, 0:HEAD_DIM], kk[:, HEAD_DIM:2 * HEAD_DIM]).astype(BF16)

        ks_sd[...] = group_cols(kst_ref)
        kw_sd[...] = group_cols(kwt_ref)
        vs_aug[0:HEAD_DIM] = vst_ref[0].astype(BF16)
        vs_aug[HEAD_DIM:HEAD_DIM + BF16_ROWS] = _ones_rows(t)
        vw_aug[0:HEAD_DIM] = vwt_ref[0].astype(BF16)
        vw_aug[HEAD_DIM:HEAD_DIM + BF16_ROWS] = _ones_rows(t)

    nu = kc_ref.shape[2]
    nsel = ovt_ref.shape[0]
    assert tq == KEY_CHUNK == WINDOW
    half = tq // 2
    r = NSA_GROUP * tq
    qb = q_ref[0]
    qt = jnp.concatenate([qb[HEAD_DIM * h:HEAD_DIM * (h + 1), a * half:(a + 1) * half]
                          for a in range(2) for h in range(NSA_GROUP)], axis=1)

    def per_head(x):
        return jnp.concatenate([x[:, a * half:(a + 1) * half] for a in range(2) for _ in range(NSA_GROUP)], axis=1)

    def head_lanes(x, h):
        return jnp.concatenate([x[:, (a * NSA_GROUP + h) * half:(a * NSA_GROUP + h + 1) * half] for a in range(2)],
                               axis=1)

    t0 = i * tq
    tpos = t0 + lax.broadcasted_iota(jnp.int32, (1, tq), 1)
    qpos = per_head(tpos)

    s = _dot(kc_ref[0, 0].astype(BF16), qt)
    cend = lax.broadcasted_iota(jnp.int32, (nu, 1), 0) * CMP_STRIDE + (CMP_BLOCK - 1)
    vis = cend <= qpos
    sm = jnp.where(vis, s, NEG_INF)
    e = jnp.where(vis, jnp.exp2(sm - jnp.max(sm, axis=0, keepdims=True)), 0.0)
    p = e / jnp.maximum(jnp.sum(e, axis=0, keepdims=True), 1e-30)
    o_cmp = _dot(vct_ref[0, 0].astype(BF16), p.astype(BF16))

    psum = head_lanes(p, 0)
    for h in range(1, NSA_GROUP):
        psum = psum + head_lanes(p, h)
    hi = psum.astype(BF16)
    lo = (psum - hi.astype(F32)).astype(BF16)
    imp = _dot(ovt_ref[...], hi) + _dot(ovt_ref[...], lo)
    blk = lax.broadcasted_iota(jnp.int32, (nsel, tq), 0)
    tl = t0 + lax.broadcasted_iota(jnp.int32, (nsel, tq), 1)
    cur = tl // SEL_BLOCK
    forced = (blk == 0) | (blk == cur) | (blk == cur - 1)
    score = jnp.where(forced, FORCE_SCORE, jnp.where(blk * SEL_BLOCK <= tl, imp, NEG_INF))
    cnt = jnp.zeros((nsel, tq), jnp.int32)
    for jp in range(nsel):
        other = score[jp:jp + 1, :]
        ahead = (other > score) | ((other == score) & (blk > jp))
        cnt = cnt + jnp.where(ahead, 1, 0)
    sel_bias = jnp.where(cnt < min(N_SELECT, nsel), 0.0, NEG_INF).astype(BF16)

    def kpos_of(off, nk):
        return off + lax.broadcasted_iota(jnp.int32, (nk, 1), 0)

    def sel_mask(off, nk):
        bias = _dot(exp_ref[pl.ds(off, nk), :], sel_bias)
        return per_head(jnp.where(kpos_of(off, nk) <= tpos, bias, NEG_INF))

    def win_mask(off, nk):
        d = tpos - kpos_of(off, nk)
        return per_head(jnp.where((d >= 0) & (d < WINDOW), 0.0, NEG_INF))

    sel_p = (0, qt, ks_sd, vs_aug, sel_mask)
    win_p = (1, qt, kw_sd, vw_aug, win_mask)
    _flash_init(m_scr, acc_scr)
    c_prev = jnp.maximum(i - 1, 0)
    _flash_run([sel_p], 0, c_prev, m_scr, acc_scr)

    def prev_chunk(c, carry):
        off = pl.multiple_of(c * KEY_CHUNK, KEY_CHUNK)
        _flash_step([sel_p], off, KEY_CHUNK, (0, r), m_scr, acc_scr)
        _flash_step([win_p], off, half, (0, r // 2), m_scr, acc_scr)
        _flash_step([win_p], off + half, half, (0, r), m_scr, acc_scr)
        return carry

    lax.fori_loop(c_prev, i, prev_chunk, 0)
    off = pl.multiple_of(i * KEY_CHUNK, KEY_CHUNK)
    _flash_step([sel_p, win_p], off, half, (0, r), m_scr, acc_scr)
    _flash_step([sel_p, win_p], off + half, half, (r // 2, r), m_scr, acc_scr)
    o_sel = _flash_finish(0, m_scr, acc_scr)
    o_win = _flash_finish(1, m_scr, acc_scr)

    for h in range(NSA_GROUP):
        def gate(br):
            return g_ref[0, pl.ds(br * NSA_HEADS + grp * NSA_GROUP + h, 1), :]

        o_ref[0, h * HEAD_DIM:(h + 1) * HEAD_DIM, :] = (
            head_lanes(o_cmp, h) * gate(0) + head_lanes(o_sel, h) * gate(1) + head_lanes(o_win, h) * gate(2)
        ).astype(BF16)


def _nsa_prompt(qn_t, g_t, kc, vct, kvs_t, kvw_t, ovt, expand_t, tq):
    nb, _, t = kvs_t.shape
    nq = t // tq
    nu = kc.shape[2]
    r = NSA_GROUP * tq
    kern = functools.partial(_nsa_prompt_kernel, tq=tq)

    def kmap(b, g, i):
        return (b, 0, 0)

    def vmap_(b, g, i):
        return (b, NSA_KV_HEADS + g, 0)

    def qmap(b, g, i):
        return (b, g, i)

    return pl.pallas_call(
        kern, grid=(nb, NSA_KV_HEADS, nq),
        in_specs=[pl.BlockSpec((1, NSA_GROUP * HEAD_DIM, tq), qmap),
                  pl.BlockSpec((1, GATE_ROWS, tq), lambda b, g, i: (b, 0, i)),
                  pl.BlockSpec((1, 1, nu, HEAD_DIM), lambda b, g, i: (b, g, 0, 0)),
                  pl.BlockSpec((1, 1, HEAD_DIM, nu), lambda b, g, i: (b, g, 0, 0)),
                  pl.BlockSpec((1, 2 * HEAD_DIM, t), kmap), pl.BlockSpec((1, HEAD_DIM, t), vmap_),
                  pl.BlockSpec((1, 2 * HEAD_DIM, t), kmap), pl.BlockSpec((1, HEAD_DIM, t), vmap_),
                  pl.BlockSpec(ovt.shape, lambda b, g, i: (0, 0)),
                  pl.BlockSpec(expand_t.shape, lambda b, g, i: (0, 0))],
        out_specs=pl.BlockSpec((1, NSA_GROUP * HEAD_DIM, tq), qmap),
        out_shape=jax.ShapeDtypeStruct((nb, NSA_WIDTH, t), BF16),
        scratch_shapes=[pltpu.VMEM((t, HEAD_DIM), BF16)] * 2 + [pltpu.VMEM((HEAD_DIM + BF16_ROWS, t), BF16)] * 2
        + [pltpu.VMEM((2, 8, r), F32), pltpu.VMEM((2, HEAD_DIM + BF16_ROWS, r), F32)],
        compiler_params=_cparams(("parallel", "arbitrary", "arbitrary")), name="nsa_prompt",
    )(qn_t, g_t, kc, vct, kvs_t, kvs_t, kvw_t, kvw_t, ovt, expand_t)


def _diff_lambda(lam_ref, lam_init):
    lq1, lk1, lq2, lk2 = lam_ref[0:1], lam_ref[1:2], lam_ref[2:3], lam_ref[3:4]
    return (jnp.exp(jnp.sum(lq1 * lk1, axis=-1, keepdims=True)) - jnp.exp(jnp.sum(lq2 * lk2, axis=-1, keepdims=True))
            + lam_init)


def _diff_prompt_kernel(q_ref, kt_ref, vt_ref, lam_ref, o_ref, k_sd, v_aug, m_scr, acc_scr, *, tq, lam_init):
    i = pl.program_id(2)
    t = kt_ref.shape[2]
    n_heads = v_aug.shape[0]

    @pl.when(i == 0)
    def _():
        kk = kt_ref[0].T
        for j in range(2 * n_heads):
            k_sd[j] = kk[:, j * DIFF_QK_DIM:(j + 1) * DIFF_QK_DIM].astype(BF16)
        for h in range(n_heads):
            v_aug[h, 0:DIFF_V_DIM] = vt_ref[0, h * DIFF_V_DIM:(h + 1) * DIFF_V_DIM, :].astype(BF16)
            v_aug[h, DIFF_V_DIM:DIFF_V_DIM + BF16_ROWS] = _ones_rows(t)

    lam = _diff_lambda(lam_ref, lam_init)
    tpos = i * tq + lax.broadcasted_iota(jnp.int32, (1, tq), 1)

    def causal(off, nk):
        kpos = off + lax.broadcasted_iota(jnp.int32, (nk, 1), 0)
        return jnp.where(kpos <= tpos, 0.0, NEG_INF)

    problems = [(j, q_ref[0, j * DIFF_QK_DIM:(j + 1) * DIFF_QK_DIM, :], k_sd.at[j], v_aug.at[j // 2], causal)
                for j in range(2 * n_heads)]
    _flash_init(m_scr, acc_scr)
    _flash_run(problems, 0, (i + 1) * (tq // KEY_CHUNK), m_scr, acc_scr)
    for h in range(n_heads):
        o_ref[0, h * DIFF_V_DIM:(h + 1) * DIFF_V_DIM, :] = (
            _flash_finish(2 * h, m_scr, acc_scr) - lam * _flash_finish(2 * h + 1, m_scr, acc_scr))


def _diff_prompt(qd_t, kd_t, vd_t, lam_vec, lam_init, tq, n_heads):
    nb, _, t = kd_t.shape
    nq = t // tq
    rows = n_heads * DIFF_V_DIM
    kern = functools.partial(_diff_prompt_kernel, tq=tq, lam_init=lam_init)
    return pl.pallas_call(
        kern, grid=(nb, DIFF_HEADS // n_heads, nq),
        in_specs=[pl.BlockSpec((1, rows, tq), lambda b, hp, i: (b, hp, i)),
                  pl.BlockSpec((1, rows, t), lambda b, hp, i: (b, hp, 0)),
                  pl.BlockSpec((1, rows, t), lambda b, hp, i: (b, hp, 0)),
                  pl.BlockSpec(lam_vec.shape, lambda b, hp, i: (0, 0))],
        out_specs=pl.BlockSpec((1, rows, tq), lambda b, hp, i: (b, hp, i)),
        out_shape=jax.ShapeDtypeStruct((nb, DIFF_WIDTH, t), F32),
        scratch_shapes=[pltpu.VMEM((2 * n_heads, t, DIFF_QK_DIM), BF16),
                        pltpu.VMEM((n_heads, DIFF_V_DIM + BF16_ROWS, t), BF16),
                        pltpu.VMEM((2 * n_heads, 8, tq), F32),
                        pltpu.VMEM((2 * n_heads, DIFF_V_DIM + BF16_ROWS, tq), F32)],
        compiler_params=_cparams(("parallel", "arbitrary", "arbitrary")), name="diff_prompt",
    )(qd_t, kd_t, vd_t, lam_vec)


def _out_kernel(x_ref, on_ref, zn_ref, od_ref, zd_ref, dow_ref, wo_ref, y_ref, *, lam_init):
    tm = x_ref.shape[0]
    o1 = on_ref[0].astype(F32) * zn_ref[0].astype(F32)
    od = od_ref[0].reshape(DIFF_HEADS, DIFF_V_DIM, tm)
    od = od * lax.rsqrt(jnp.mean(od * od, axis=1, keepdims=True) + NORM_EPS)
    od = (od * dow_ref[...].reshape(1, DIFF_V_DIM, 1) * (1.0 - lam_init)).reshape(DIFF_WIDTH, tm)
    o2 = od * zd_ref[0].astype(F32)
    o = jnp.concatenate([o1, o2], axis=0).T.astype(BF16)
    y_ref[...] = x_ref[...] + _dot(o, wo_ref[...])


def _out_proj(x2, o_nsa_t, zn_t, od_t, zd_t, prm, lam_init, tm):
    n, d = x2.shape
    t = o_nsa_t.shape[2]
    nt = t // tm

    def tok(i):
        return (i // nt, 0, i % nt)

    def full(a):
        return pl.BlockSpec(a.shape, lambda i: (0, 0))

    consts = (prm['dow_col'], prm['w_out'])
    return pl.pallas_call(
        functools.partial(_out_kernel, lam_init=lam_init), grid=(n // tm,),
        in_specs=[pl.BlockSpec((tm, d), lambda i: (i, 0))] + [pl.BlockSpec((1, 512, tm), tok)] * 4
        + [full(a) for a in consts],
        out_specs=pl.BlockSpec((tm, d), lambda i: (i, 0)), out_shape=jax.ShapeDtypeStruct((n, d), F32),
        compiler_params=_cparams(("parallel",)), name="out_proj",
    )(x2, o_nsa_t, zn_t, od_t, zd_t, *consts)


def _mix_out(x, o_nsa, zn, od, zd, g64_ref, dow_ref, wo_ref, lam_init):
    o1 = (o_nsa * zn.astype(F32)).astype(BF16)
    odn = od * lax.rsqrt(_dot(od * od, g64_ref[...]) + NORM_EPS) * dow_ref[...] * (1.0 - lam_init)
    o2 = (odn * zd.astype(F32)).astype(BF16)
    return x + _dot(o1, wo_ref[0:NSA_WIDTH]) + _dot(o2, wo_ref[NSA_WIDTH:NSA_WIDTH + DIFF_WIDTH])


def _sample_cmp_kernel(pt_ref, q_ref, *rest, pps, past, n_sel):
    pages = rest[:pps]
    (w1_ref, posl_ref, w2_ref, g64_ref, kcw_ref, cc_ref, sac_ref, sbc_ref, ov_ref, ocmp_ref, idx_ref,
     xs_ref) = rest[pps:]
    s = pl.program_id(1)
    for k in range(pps):
        row0 = pl.multiple_of((s * pps + k) * LANES, LANES)
        for c in range(2):
            xs_ref[c, pl.ds(row0, LANES), :] = pages[k][0, c * 128:(c + 1) * 128, :].T

    @pl.when(s == pl.num_programs(1) - 1)
    def _():
        nu = past // CMP_STRIDE
        nuc = min(nu, 256)
        outs = []
        for c in range(2):
            pq = jnp.concatenate([_compress_units(xs_ref, nu, u0, nuc, w1_ref, posl_ref, w2_ref, c)
                                  for u0 in range(0, nu, nuc)], axis=0)
            outs.append(_compress_finish(pq, posl_ref, w1_ref, w2_ref, c))
        kct = _kc_norm_rope(outs[0], g64_ref, kcw_ref, cc_ref, sac_ref, sbc_ref).T.astype(BF16)
        vc = outs[1].astype(BF16)
        q = q_ref[0]
        row = lax.broadcasted_iota(jnp.int32, (NSA_HEADS, 1), 0)
        first = row < NSA_GROUP
        sc = jnp.where(first, _dot(q, kct[0:HEAD_DIM]), _dot(q, kct[HEAD_DIM:2 * HEAD_DIM]))
        cend = lax.broadcasted_iota(jnp.int32, (1, nu), 1) * CMP_STRIDE + (CMP_BLOCK - 1)
        p = _softmax_rows(sc, cend <= past)
        pb = p.astype(BF16)
        ocmp_ref[0] = jnp.where(first, _dot(pb, vc[:, 0:HEAD_DIM]), _dot(pb, vc[:, HEAD_DIM:2 * HEAD_DIM]))
        psum = jnp.concatenate([jnp.sum(p[0:NSA_GROUP], axis=0, keepdims=True),
                                jnp.sum(p[NSA_GROUP:], axis=0, keepdims=True),
                                jnp.zeros((NSA_HEADS - 2, nu), F32)], axis=0)
        hi = psum.astype(BF16)
        lo = (psum - hi.astype(F32)).astype(BF16)
        imp = _dot(hi, ov_ref[...]) + _dot(lo, ov_ref[...])
        nselp = ov_ref.shape[1]
        blk = lax.broadcasted_iota(jnp.int32, (NSA_HEADS, nselp), 1)
        cur = past // SEL_BLOCK
        forced = (blk == 0) | (blk == cur) | (blk == cur - 1)
        score = jnp.where(forced, FORCE_SCORE, jnp.where(blk * SEL_BLOCK <= past, imp, NEG_INF))
        score = jnp.where(blk < n_sel, score, EXCLUDED)
        blkf = blk.astype(F32)
        lane = lax.broadcasted_iota(jnp.int32, (NSA_HEADS, LANES), 1)
        picked = jnp.zeros((NSA_HEADS, LANES), F32)
        for k in range(min(N_SELECT, n_sel)):
            mx = jnp.max(score, axis=-1, keepdims=True)
            ik = jnp.min(jnp.where(score == mx, blkf, 1e9), axis=-1, keepdims=True)
            picked = jnp.where(lane == k, ik, picked)
            score = jnp.where(blkf == ik, EXCLUDED, score)
        idx_ref[0] = picked.astype(jnp.int32)


def _sample_cmp(page_table, q8, cache_ct, prm, cend_tables, ov, past, n_sel, pps):
    db, n_pages = page_table.shape
    n_steps = n_pages // pps
    consts = (prm['cmp_w1'], prm['cmp_posl'], prm['cmp_w2'], prm['g64_128'], prm['kcw']) + tuple(cend_tables) + (ov,)

    def full(a):
        nd = a.ndim
        return pl.BlockSpec(a.shape, lambda b, s, pt: (0,) * nd)

    def page_spec(k):
        return pl.BlockSpec((1, 256, LANES), lambda b, s, pt: (pt[b, s * pps + k], 0, 0))

    kern = functools.partial(_sample_cmp_kernel, pps=pps, past=past, n_sel=n_sel)
    return pl.pallas_call(
        kern,
        grid_spec=pltpu.PrefetchScalarGridSpec(
            num_scalar_prefetch=1, grid=(db, n_steps),
            in_specs=[pl.BlockSpec((1, NSA_HEADS, HEAD_DIM), lambda b, s, pt: (b, 0, 0))]
            + [page_spec(k) for k in range(pps)] + [full(a) for a in consts],
            out_specs=[pl.BlockSpec((1, NSA_HEADS, HEAD_DIM), lambda b, s, pt: (b, 0, 0)),
                       pl.BlockSpec((1, NSA_HEADS, LANES), lambda b, s, pt: (b, 0, 0))],
            scratch_shapes=[pltpu.VMEM((2, past, LANES), F32)]),
        out_shape=[jax.ShapeDtypeStruct((db, NSA_HEADS, HEAD_DIM), F32),
                   jax.ShapeDtypeStruct((db, NSA_HEADS, LANES), jnp.int32)],
        compiler_params=_cparams(("parallel", "arbitrary")), name="sample_cmp",
    )(page_table, q8, *([cache_ct] * pps), *consts)


def _sample_attn_kernel(pt_ref, idx_ref, q_ref, knew_ref, vnew_ref, wnew_ref, win_ref, *rest, n_k, past, n_pages):
    pages = rest[:n_k]
    osel_ref, owin_ref, wout_ref = rest[n_k:]
    b = pl.program_id(0)
    g = pl.program_id(1)
    q = q_ref[0, 0]
    lane = lax.broadcasted_iota(jnp.int32, (1, LANES), 1)
    n_cached = 2 * n_pages

    ss, vts = [], []
    for k in range(n_k):
        j = idx_ref[b, g, k]
        jc = jnp.minimum(j, n_cached - 1)
        kpos = (jc // 2) * LANES + lane
        ok = (lane // SEL_BLOCK == jc % 2) & (kpos <= past) & (j < n_cached)
        ss.append(jnp.where(ok, _dot(q, pages[k][0, 0, 0].astype(BF16)), NEG_INF))
        vts.append(pages[k][0, 1, 0].astype(BF16))
    s_new = jnp.sum(q.astype(F32) * knew_ref[0, 0][0:1], axis=-1, keepdims=True)
    m = s_new
    for s in ss:
        m = jnp.maximum(m, jnp.max(s, axis=-1, keepdims=True))
    e_new = jnp.exp(s_new - m)
    l = e_new
    acc = e_new * vnew_ref[0, 0][0:1]
    for s, vt in zip(ss, vts):
        e = jnp.exp(s - m)
        l = l + jnp.sum(e, axis=-1, keepdims=True)
        acc = acc + _dot_nt(e.astype(BF16), vt)
    osel_ref[0, 0] = acc / jnp.maximum(l, 1e-30)

    w = win_ref.shape[4]
    wl = lax.broadcasted_iota(jnp.int32, (1, w), 1)
    kt = jnp.where(wl == w - 1, wnew_ref[0, 0, 0], pltpu.roll(win_ref[0, 0, 0], w - 1, 1))
    vt = jnp.where(wl == w - 1, wnew_ref[0, 1, 0], pltpu.roll(win_ref[0, 1, 0], w - 1, 1))
    wout_ref[0, 0, 0] = kt
    wout_ref[0, 1, 0] = vt
    kpos = past - (w - 1) + wl
    d = past - kpos
    p = _softmax_rows(_dot(q, kt.astype(BF16)), (d >= 0) & (d < WINDOW) & (kpos >= 0))
    owin_ref[0, 0] = _dot_nt(p.astype(BF16), vt.astype(BF16))


def _sample_attn(page_table, idx, q4, knew, vnew, wnew, win_t, cache_st, past):
    db, n_pages = page_table.shape
    n_k = idx.shape[2]
    w = win_t.shape[4]

    def per_bg(shape):
        nd = len(shape)
        return pl.BlockSpec((1, 1) + shape, lambda b, g, pt, ix: (b, g) + (0,) * nd)

    def kv_bg(last):
        return pl.BlockSpec((1, 2, 1, HEAD_DIM, last), lambda b, g, pt, ix: (b, 0, g, 0, 0))

    def page_spec(k):
        def imap(b, g, pt, ix):
            j = jnp.minimum(ix[b, g, k], 2 * n_pages - 1)
            return (pt[b, j // 2], 0, g, 0, 0)
        return pl.BlockSpec((1, 2, 1, HEAD_DIM, LANES), imap)

    kern = functools.partial(_sample_attn_kernel, n_k=n_k, past=past, n_pages=n_pages)
    o_shape = jax.ShapeDtypeStruct((db, NSA_KV_HEADS, 8, HEAD_DIM), F32)
    return pl.pallas_call(
        kern,
        grid_spec=pltpu.PrefetchScalarGridSpec(
            num_scalar_prefetch=2, grid=(db, NSA_KV_HEADS),
            in_specs=[per_bg((8, HEAD_DIM))] * 3 + [kv_bg(1), kv_bg(w)] + [page_spec(k) for k in range(n_k)],
            out_specs=[per_bg((8, HEAD_DIM)), per_bg((8, HEAD_DIM)), kv_bg(w)]),
        out_shape=[o_shape, o_shape, jax.ShapeDtypeStruct(win_t.shape, F32)],
        compiler_params=_cparams(("parallel", "arbitrary")), name="sample_attn",
    )(page_table, idx, q4, knew, vnew, wnew, win_t, *([cache_st] * n_k))


def _sample_diff_kernel(pt_ref, qbd_ref, q16_ref, k16_ref, vnew_ref, lam_ref, *rest, pps, lam_init):
    kpages = rest[:pps]
    vpages = rest[pps:2 * pps]
    o_ref, m_scr, l_scr, acc_scr = rest[2 * pps:]
    s = pl.program_id(1)
    rows = 2 * DIFF_HEADS

    @pl.when(s == 0)
    def _():
        m_scr[...] = jnp.full(m_scr.shape, NEG_INF, F32)
        l_scr[...] = jnp.zeros(l_scr.shape, F32)
        acc_scr[...] = jnp.zeros(acc_scr.shape, F32)

    qbd = qbd_ref[0]
    ss = [_dot(qbd, kp[0].astype(BF16)) for kp in kpages]
    cm = ss[0]
    for sk in ss[1:]:
        cm = jnp.maximum(cm, sk)
    m_old = m_scr[...]
    m_new = jnp.maximum(m_old, jnp.max(cm, axis=-1, keepdims=True))
    alpha = jnp.exp(m_old - m_new)
    lsum = jnp.zeros((rows, LANES), F32)
    accs = jnp.zeros(acc_scr.shape, F32)
    for sk, vp in zip(ss, vpages):
        e = jnp.exp(sk - m_new)
        lsum = lsum + e
        accs = accs + _dot_nt(e.astype(BF16), vp[0].astype(BF16))
    m_scr[...] = m_new
    l_scr[...] = alpha * l_scr[...] + jnp.sum(lsum, axis=-1, keepdims=True)
    acc_scr[...] = alpha[:, 0:1] * acc_scr[...] + accs

    @pl.when(s == pl.num_programs(1) - 1)
    def _():
        s_new = jnp.sum(q16_ref[0] * k16_ref[0], axis=-1, keepdims=True)
        m1 = m_scr[...][:, 0:1]
        m2 = jnp.maximum(m1, s_new)
        a2 = jnp.exp(m1 - m2)
        e_new = jnp.exp(s_new - m2)
        l = a2 * l_scr[...][:, 0:1] + e_new
        acc = acc_scr[...]
        col = lax.broadcasted_iota(jnp.int32, acc.shape, 1) // DIFF_V_DIM
        rowh = lax.broadcasted_iota(jnp.int32, acc.shape, 0) % DIFF_HEADS
        acc = jnp.where(col == rowh, acc, 0.0)
        o16 = acc[:, 0:DIFF_V_DIM]
        for h in range(1, DIFF_HEADS):
            o16 = o16 + acc[:, h * DIFF_V_DIM:(h + 1) * DIFF_V_DIM]
        vnew = vnew_ref[0]
        o16 = (a2 * o16 + e_new * jnp.concatenate([vnew, vnew], axis=0)) / jnp.maximum(l, 1e-30)
        o_ref[0] = o16[0:DIFF_HEADS] - _diff_lambda(lam_ref, lam_init) * o16[DIFF_HEADS:]


def _sample_diff(page_table, qbd, q16, k16, vnew, lam_vec, cache_dkt, cache_dvt, lam_init, pps):
    db, n_pages = page_table.shape
    n_steps = n_pages // pps
    rows = 2 * DIFF_HEADS

    def per_b(shape):
        nd = len(shape)
        return pl.BlockSpec((1,) + shape, lambda b, s, pt: (b,) + (0,) * nd)

    def page_spec(k):
        return pl.BlockSpec((1, 512, LANES), lambda b, s, pt: (pt[b, s * pps + k], 0, 0))

    kern = functools.partial(_sample_diff_kernel, pps=pps, lam_init=lam_init)
    return pl.pallas_call(
        kern,
        grid_spec=pltpu.PrefetchScalarGridSpec(
            num_scalar_prefetch=1, grid=(db, n_steps),
            in_specs=[per_b((rows, 512)), per_b((rows, DIFF_QK_DIM)), per_b((rows, DIFF_QK_DIM)),
                      per_b((DIFF_HEADS, DIFF_V_DIM)), pl.BlockSpec(lam_vec.shape, lambda b, s, pt: (0, 0))]
            + [page_spec(k) for k in range(pps)] * 2,
            out_specs=per_b((DIFF_HEADS, DIFF_V_DIM)),
            scratch_shapes=[pltpu.VMEM((rows, LANES), F32), pltpu.VMEM((rows, LANES), F32),
                            pltpu.VMEM((rows, 512), F32)]),
        out_shape=jax.ShapeDtypeStruct((db, DIFF_HEADS, DIFF_V_DIM), F32),
        compiler_params=_cparams(("parallel", "arbitrary")), name="sample_diff",
    )(page_table, qbd, q16, k16, vnew, lam_vec, *([cache_dkt] * pps), *([cache_dvt] * pps))


def _sample_out_kernel(x_ref, oc_ref, os_ref, ow_ref, g0_ref, g1_ref, g2_ref, zn_ref, od_ref, zd_ref, g64_ref, dow_ref,
                       wo_ref, y_ref, *, lam_init):
    o_nsa = oc_ref[...] * g0_ref[...] + os_ref[...] * g1_ref[...] + ow_ref[...] * g2_ref[...]
    y_ref[...] = _mix_out(x_ref[...], o_nsa, zn_ref[...], od_ref[...], zd_ref[...], g64_ref, dow_ref, wo_ref, lam_init)


def _sample_out(x2, branches, gates, zn, od, zd, prm, lam_init):
    args = (x2,) + tuple(branches) + tuple(gates) + (zn, od, zd, prm['g64'], prm['dow'], prm['w_out'])
    return pl.pallas_call(
        functools.partial(_sample_out_kernel, lam_init=lam_init),
        out_shape=jax.ShapeDtypeStruct(x2.shape, F32), compiler_params=_cparams(None), name="sample_out",
    )(*args)


def _layer_params(lp):
    w = lp['w_in']
    d = w.shape[0]
    cols = [w[:, _OFFS[k]:_OFFS[k + 1]] for k in _ROW_ORDER]
    cols.append(jnp.zeros((d, GATE_ROWS - _SIZES[4]), F32))
    w_t = jnp.concatenate(cols, axis=1).T

    def cmp_w1(w1):
        w1 = w1.reshape(2, CMP_STRIDE, HEAD_DIM, HEAD_DIM)
        eye = jnp.eye(NSA_KV_HEADS, dtype=F32)
        full = jnp.einsum('hlde,gk->lgdhke', w1, eye)
        return full.reshape(CMP_STRIDE * NSA_KV_HEADS * HEAD_DIM, 2 * NSA_KV_HEADS * HEAD_DIM)

    def cmp_pos(pos):
        pos = pos.reshape(2, CMP_STRIDE, 1, HEAD_DIM)
        rows = jnp.broadcast_to(pos, (2, CMP_STRIDE, NSA_KV_HEADS, HEAD_DIM)).reshape(2, -1)
        return jnp.concatenate([rows, jnp.zeros((6, rows.shape[1]), F32)], axis=0)

    def cmp_w2(w2):
        return jnp.kron(jnp.eye(NSA_KV_HEADS, dtype=F32), w2)

    lam_vec = jnp.concatenate([lp['lambda_q1'][None], lp['lambda_k1'][None], lp['lambda_q2'][None],
                               lp['lambda_k2'][None], jnp.zeros((4, DIFF_QK_DIM), F32)], axis=0)
    return {
        'norm_w': lp['norm_w'][None], 'w_t': w_t.astype(BF16),
        'g64': _group_mean_matrix(512, HEAD_DIM), 'g64_128': _group_mean_matrix(LANES, HEAD_DIM),
        'qnw': lp['nsa_q_norm'][:, None], 'qdw': lp['diff_q_norm'][:, None],
        'ksw': lp['nsa_ks_norm'][:, None], 'kww': lp['nsa_kw_norm'][:, None], 'kdw': lp['diff_k_norm'][:, None],
        'kcw': jnp.tile(lp['nsa_kc_norm'], NSA_KV_HEADS)[None],
        'cmp_w1': jnp.stack([cmp_w1(lp['cmp_w1_k']), cmp_w1(lp['cmp_w1_v'])]).astype(BF16),
        'cmp_posl': jnp.stack([cmp_pos(lp['cmp_pos_k']), cmp_pos(lp['cmp_pos_v'])]).astype(BF16),
        'cmp_w2': jnp.stack([cmp_w2(lp['cmp_w2_k']), cmp_w2(lp['cmp_w2_v'])]).astype(BF16),
        'dow': jnp.tile(lp['diff_out_norm'], DIFF_HEADS)[None], 'dow_col': lp['diff_out_norm'][:, None],
        'w_out': lp['w_out'].astype(BF16), 'lam_vec': lam_vec,
    }


def _overlap_t(n_sel_pad, nu, n_cmp, n_sel):
    cs = np.arange(nu)[None, :] * CMP_STRIDE
    ss = np.arange(n_sel_pad)[:, None] * SEL_BLOCK
    ov = np.clip(np.minimum(cs + CMP_BLOCK, ss + SEL_BLOCK) - np.maximum(cs, ss), 0, None) / CMP_STRIDE
    ov = ov * (np.arange(nu)[None, :] < n_cmp) * (np.arange(n_sel_pad)[:, None] < n_sel)
    return jnp.asarray(ov, dtype=BF16)


def _prompt_layer(x, prm, lam_init):
    b, t, d = x.shape
    x2 = x.reshape(b * t, d)
    tables = _rope_tables(np.arange(t))
    qn_t, kvc_t, kvs_t, kvw_t, zn_t, qd_t, kd_t, vd_t, zd_t, g_t = _project(x2, tables, t, prm, 512, True)
    nu = t // CMP_STRIDE
    n_cmp = nu - CMP_BLOCK // CMP_STRIDE + 1
    cend_tables = _rope_tables(np.arange(nu) * CMP_STRIDE + CMP_BLOCK - 1)[0]
    kc, vct = _compress_prompt(kvc_t, prm, cend_tables)
    n_sel = -(-t // SEL_BLOCK)
    ovt = _overlap_t(n_sel, nu, n_cmp, n_sel)
    expand_t = jnp.asarray(np.arange(t)[:, None] // SEL_BLOCK == np.arange(n_sel)[None, :], dtype=BF16)
    o_nsa_t = _nsa_prompt(qn_t, g_t, kc, vct, kvs_t, kvw_t, ovt, expand_t, 512)
    o_d_t = _diff_prompt(qd_t, kd_t, vd_t, prm['lam_vec'], lam_init, 512, 8)
    y = _out_proj(x2, o_nsa_t, zn_t, o_d_t, zd_t, prm, lam_init, 512).reshape(b, t, d)

    def state(a_t, shape_tail):
        return jnp.transpose(a_t.reshape((b,) + shape_tail + (a_t.shape[-1],)),
                             (0, len(shape_tail) + 1) + tuple(range(1, len(shape_tail) + 1)))

    kv_tail = (2, NSA_KV_HEADS, HEAD_DIM)
    w = min(WINDOW, t)
    return y, (state(kvc_t, kv_tail), state(kvs_t, kv_tail), state(kvw_t[:, :, t - w:], kv_tail),
               state(kd_t, (DIFF_HEADS, 2, DIFF_QK_DIM)), state(vd_t, (DIFF_HEADS, DIFF_V_DIM)))


def _sample_layer(x, cache_c, cache_s, cache_dk, cache_dv, win, page_table, prm, lam_init):
    db, t, d = x.shape
    assert t == 1 and db % 8 == 0
    n_pool, page = cache_c.shape[:2]
    assert page == LANES
    n_pages = page_table.shape[1]
    past = n_pages * page
    x2 = x.reshape(db, d)
    dbp = -(-db // LANES) * LANES
    tables = _rope_tables(np.full((dbp,), past))
    proj = _project(jnp.pad(x2, ((0, dbp - db), (0, 0))), tables, dbp, prm, dbp, False)
    qn, kvc, kvs, kvw, zn, qd, kd, vd, zd, gn = [a[0, :, :db].T for a in proj]

    cache_ct = jnp.transpose(cache_c, (0, 2, 3, 4, 1)).reshape(n_pool, 256, page)
    cache_st = jnp.transpose(cache_s, (0, 2, 3, 4, 1))
    cache_dkt = jnp.transpose(cache_dk, (0, 2, 3, 4, 1)).reshape(n_pool, 512, page)
    cache_dvt = jnp.transpose(cache_dv, (0, 2, 3, 1)).reshape(n_pool, 512, page)
    win_t = jnp.transpose(win, (0, 2, 3, 4, 1))

    nu = past // CMP_STRIDE
    n_cmp = nu - CMP_BLOCK // CMP_STRIDE + 1
    n_sel = -(-(past + t) // SEL_BLOCK)
    n_sel_pad = -(-n_sel // LANES) * LANES
    ov = _overlap_t(n_sel_pad, nu, n_cmp, n_sel).T
    cend_tables = _rope_tables(np.arange(nu) * CMP_STRIDE + CMP_BLOCK - 1)[0]
    pps = min(64, n_pages)
    o_cmp, picked = _sample_cmp(page_table, qn.reshape(db, NSA_HEADS, HEAD_DIM), cache_ct, prm, cend_tables, ov, past,
                                n_sel, pps)
    n_k = min(N_SELECT, n_sel)
    idx = picked[:, :NSA_KV_HEADS, :n_k]

    def rows8(a):
        return jnp.pad(a, ((0, 0), (0, 0), (0, 8 - a.shape[2]), (0, 0)))

    kvs5 = kvs.reshape(db, 2, NSA_KV_HEADS, 1, HEAD_DIM)
    q4 = rows8(qn.reshape(db, NSA_KV_HEADS, NSA_GROUP, HEAD_DIM))
    o_sel, o_win, win_new = _sample_attn(page_table, idx, q4, rows8(kvs5[:, 0]), rows8(kvs5[:, 1]),
                                         kvw.reshape(db, 2, NSA_KV_HEADS, HEAD_DIM, 1), win_t, cache_st, past)

    qd3 = qd.astype(F32).reshape(db, DIFF_HEADS, 2, DIFF_QK_DIM)
    q16 = jnp.transpose(qd3, (0, 2, 1, 3)).reshape(db, 2 * DIFF_HEADS, DIFF_QK_DIM)
    own = np.zeros((2 * DIFF_HEADS, DIFF_QK_COLS), np.float32)
    for c in range(2):
        for h in range(DIFF_HEADS):
            lo = h * 2 * DIFF_QK_DIM + c * DIFF_QK_DIM
            own[c * DIFF_HEADS + h, lo:lo + DIFF_QK_DIM] = 1.0
    qbd = (qd.astype(F32)[:, None, :] * own[None]).astype(BF16)
    kd3 = kd.reshape(db, DIFF_HEADS, 2, DIFF_QK_DIM)
    k16 = jnp.transpose(kd3, (0, 2, 1, 3)).reshape(db, 2 * DIFF_HEADS, DIFF_QK_DIM)
    vnew = vd.reshape(db, DIFF_HEADS, DIFF_V_DIM)
    o_d = _sample_diff(page_table, qbd, q16, k16, vnew, prm['lam_vec'], cache_dkt, cache_dvt, lam_init,
                       min(32, n_pages))

    gates = [jnp.repeat(gn[:, br * NSA_HEADS:(br + 1) * NSA_HEADS], HEAD_DIM, axis=1) for br in range(3)]
    branches = (o_cmp.reshape(db, NSA_WIDTH), o_sel[:, :, :NSA_GROUP].reshape(db, NSA_WIDTH),
                o_win[:, :, :NSA_GROUP].reshape(db, NSA_WIDTH))
    y = _sample_out(x2, branches, gates, zn, o_d.reshape(db, DIFF_WIDTH), zd, prm, lam_init).reshape(db, t, d)

    kv_shape = (db, t, 2, NSA_KV_HEADS, HEAD_DIM)
    states = (kvc.reshape(kv_shape), kvs.reshape(kv_shape), jnp.transpose(win_new, (0, 4, 1, 2, 3)),
              kd3.reshape(db, t, DIFF_HEADS, 2, DIFF_QK_DIM), vnew.reshape(db, t, DIFF_HEADS, DIFF_V_DIM))
    return y, states


def kernel(x_prompt, x_sample, cache_nsa_cmp_kv, cache_nsa_sel_kv, cache_diff_k, cache_diff_v, state_nsa_win_kv, page_table, norm_w, w_in, nsa_q_norm, nsa_kc_norm, nsa_ks_norm, nsa_kw_norm, cmp_pos_k, cmp_w1_k, cmp_w2_k, cmp_pos_v, cmp_w1_v, cmp_w2_v, diff_q_norm, diff_k_norm, lambda_q1, lambda_k1, lambda_q2, lambda_k2, diff_out_norm, w_out):
    depth = w_in.shape[0]
    y_p, y_s = x_prompt, x_sample
    p_states, s_states = [], []
    for layer in range(depth):
        lp = {
            'norm_w': norm_w[layer], 'w_in': w_in[layer],
            'nsa_q_norm': nsa_q_norm[layer], 'nsa_kc_norm': nsa_kc_norm[layer],
            'nsa_ks_norm': nsa_ks_norm[layer], 'nsa_kw_norm': nsa_kw_norm[layer],
            'cmp_pos_k': cmp_pos_k[layer], 'cmp_w1_k': cmp_w1_k[layer], 'cmp_w2_k': cmp_w2_k[layer],
            'cmp_pos_v': cmp_pos_v[layer], 'cmp_w1_v': cmp_w1_v[layer], 'cmp_w2_v': cmp_w2_v[layer],
            'diff_q_norm': diff_q_norm[layer], 'diff_k_norm': diff_k_norm[layer],
            'lambda_q1': lambda_q1[layer], 'lambda_k1': lambda_k1[layer],
            'lambda_q2': lambda_q2[layer], 'lambda_k2': lambda_k2[layer],
            'diff_out_norm': diff_out_norm[layer], 'w_out': w_out[layer],
        }
        lam_init = 0.8 - 0.6 * math.exp(-0.3 * layer)
        prm = _layer_params(lp)
        y_p, ps = _prompt_layer(y_p, prm, lam_init)
        y_s, ss = _sample_layer(y_s, cache_nsa_cmp_kv[layer], cache_nsa_sel_kv[layer], cache_diff_k[layer],
                                cache_diff_v[layer], state_nsa_win_kv[layer], page_table, prm, lam_init)
        p_states.append(ps)
        s_states.append(ss)
    p_c, p_s, p_w, p_dk, p_dv = [jnp.stack(t, axis=0) for t in zip(*p_states)]
    s_c, s_s, s_w, s_dk, s_dv = [jnp.stack(t, axis=0) for t in zip(*s_states)]
    return (y_p, y_s, p_c, p_s, p_w, p_dk, p_dv, s_c, s_s, s_w, s_dk, s_dv)
```

```python
import functools
import math

import jax
import jax.numpy as jnp
import numpy as np
from jax import lax
from jax.experimental import pallas as pl
from jax.experimental.pallas import tpu as pltpu

HEAD_DIM = 64
NSA_HEADS = 8
NSA_KV_HEADS = 2
NSA_GROUP = NSA_HEADS // NSA_KV_HEADS
NSA_WIDTH = NSA_HEADS * HEAD_DIM
CMP_BLOCK = 32
CMP_STRIDE = 16
SEL_BLOCK = 64
N_SELECT = 16
WINDOW = 512
DIFF_HEADS = 8
DIFF_QK_DIM = 32
DIFF_V_DIM = 64
DIFF_WIDTH = DIFF_HEADS * DIFF_V_DIM
KV_COLS = 2 * NSA_KV_HEADS * HEAD_DIM
DIFF_QK_COLS = DIFF_HEADS * 2 * DIFF_QK_DIM
ROPE_THETA = 500000.0
ROT_FRACTION = 4
NORM_EPS = 1e-6
NEG_INF = -1e30
FORCE_SCORE = 1e9
EXCLUDED = -3e38
LOG2E = 1.4426950408889634

LANES = 128
BF16_ROWS = 16
KEY_CHUNK = 512
VMEM_LIMIT = 56 * 1024 * 1024

F32 = jnp.float32
BF16 = jnp.bfloat16

_SIZES = (NSA_WIDTH, KV_COLS, KV_COLS, KV_COLS, 3 * NSA_HEADS, NSA_WIDTH, DIFF_QK_COLS, DIFF_QK_COLS, DIFF_WIDTH,
          DIFF_WIDTH)
_OFFS = tuple(int(v) for v in np.concatenate([[0], np.cumsum(_SIZES)]))
_ROW_ORDER = (0, 1, 2, 3, 5, 6, 7, 8, 9, 4)
GATE_ROWS = 32
_ROW_SIZES = tuple(_SIZES[k] for k in _ROW_ORDER[:-1]) + (GATE_ROWS,)
_ROW_OFFS = tuple(int(v) for v in np.concatenate([[0], np.cumsum(_ROW_SIZES)]))
PROJ_ROWS = _ROW_OFFS[-1]


def _cparams(sem):
    return pltpu.CompilerParams(dimension_semantics=sem, vmem_limit_bytes=VMEM_LIMIT)


def _sigmoid(x):
    return 1.0 / (1.0 + jnp.exp(-x))


def _dot(a, b):
    return jnp.dot(a, b, preferred_element_type=F32)


def _dot_nt(a, b):
    return lax.dot_general(a, b, (((1,), (1,)), ((), ())), preferred_element_type=F32)


def _rope_lanes(x, c, sa, sb, half):
    parts = []
    for k in range(x.shape[1] // LANES):
        xk = x[:, LANES * k:LANES * (k + 1)]
        parts.append(xk * c + pltpu.roll(xk, LANES - half, 1) * sa + pltpu.roll(xk, half, 1) * sb)
    return jnp.concatenate(parts, axis=1) if len(parts) > 1 else parts[0]


def _norm_rope_rows(x, w_ref, c, s, groups, gd):
    tm = x.shape[1]
    x3 = x.reshape(groups, gd, tm)
    r = lax.rsqrt(jnp.mean(x3 * x3, axis=1, keepdims=True) + NORM_EPS)
    x3 = (x3 * r) * w_ref[...].reshape(1, gd, 1)
    if gd == HEAD_DIM:
        x1 = x3[:, 0:8]
        x2 = x3[:, 8:16]
        parts = [x1 * c - x2 * s, x2 * c + x1 * s, x3[:, 16:]]
    else:
        rot = x3[:, 0:8].reshape(groups * 8, tm)
        first = lax.broadcasted_iota(jnp.int32, rot.shape, 0) % 8 < 4
        swapped = jnp.where(first, pltpu.roll(rot, rot.shape[0] - 4, 0), pltpu.roll(rot, 4, 0))
        parts = [rot.reshape(groups, 8, tm) * c + swapped.reshape(groups, 8, tm) * s, x3[:, 8:]]
    return jnp.concatenate(parts, axis=1).reshape(groups * gd, tm)


def _proj_kernel(x_ref, nw_ref, wt_ref, qnw_ref, ksw_ref, kww_ref, qdw_ref, kdw_ref, c8_ref, s8_ref, c4_ref, s4_ref,
                 qn_ref, kvc_ref, kvs_ref, kvw_ref, zn_ref, qd_ref, kd_ref, vd_ref, zd_ref, g_ref, *, qn_scale,
                 qd_scale):
    x = x_ref[...]
    ms = jnp.mean(x * x, axis=-1, keepdims=True)
    ht = ((x * lax.rsqrt(ms + NORM_EPS)) * nw_ref[...]).T.astype(BF16)

    def rows(k):
        return _dot(wt_ref[_ROW_OFFS[k]:_ROW_OFFS[k + 1]], ht)

    c8, s8, c4, s4 = c8_ref[...], s8_ref[...], c4_ref[...], s4_ref[...]
    qn_ref[0] = (_norm_rope_rows(rows(0), qnw_ref, c8, s8, NSA_HEADS, HEAD_DIM) * qn_scale).astype(BF16)
    kvc_ref[0] = rows(1)
    kvs = rows(2)
    kvs_ref[0, 0:128] = _norm_rope_rows(kvs[0:128], ksw_ref, c8, s8, NSA_KV_HEADS, HEAD_DIM)
    kvs_ref[0, 128:256] = kvs[128:256]
    kvw = rows(3)
    kvw_ref[0, 0:128] = _norm_rope_rows(kvw[0:128], kww_ref, c8, s8, NSA_KV_HEADS, HEAD_DIM)
    kvw_ref[0, 128:256] = kvw[128:256]
    zn = rows(4)
    zn_ref[0] = (zn * _sigmoid(zn)).astype(BF16)
    qd_ref[0] = (_norm_rope_rows(rows(5), qdw_ref, c4, s4, 2 * DIFF_HEADS, DIFF_QK_DIM) * qd_scale).astype(BF16)
    kd_ref[0] = _norm_rope_rows(rows(6), kdw_ref, c4, s4, 2 * DIFF_HEADS, DIFF_QK_DIM)
    vd_ref[0] = rows(7)
    zd = rows(8)
    zd_ref[0] = (zd * _sigmoid(zd)).astype(BF16)
    g_ref[0] = _sigmoid(rows(9))


def _rope_tables(pos):
    pos = np.asarray(pos, np.float32)
    lane = np.arange(LANES)

    def tables(group, half):
        inv = np.float32(ROPE_THETA) ** (-np.arange(half, dtype=np.float32) / np.float32(half))
        ang = (pos[:, None] * inv).astype(np.float32)
        cos, sin = np.cos(ang), np.sin(ang)
        m = lane % group
        cosl, sinl = cos[:, m % half], sin[:, m % half]
        c = np.where(m < 2 * half, cosl, 1.0)
        sa = np.where(m < half, -sinl, 0.0)
        sb = np.where((m >= half) & (m < 2 * half), sinl, 0.0)
        return tuple(jnp.asarray(a, F32) for a in (c, sa, sb)), (cos.T, sin.T)

    tq, (c8, s8) = tables(HEAD_DIM, HEAD_DIM // ROT_FRACTION // 2)
    td, (c4, s4) = tables(DIFF_QK_DIM, DIFF_QK_DIM // ROT_FRACTION // 2)
    c4 = np.concatenate([c4, c4], axis=0)
    s4 = np.concatenate([-s4, s4], axis=0)
    return tq, td, tuple(jnp.asarray(a, F32) for a in (c8, s8)), tuple(jnp.asarray(a, F32) for a in (c4, s4))


def _group_mean_matrix(width, group):
    idx = np.arange(width)
    return jnp.asarray((idx[:, None] // group == idx[None, :] // group) / group, dtype=F32)


def _project(x2, pos_tables, t_per_batch, prm, tm, log2e_in_q):
    n, d = x2.shape
    nb = n // t_per_batch
    nt = t_per_batch // tm
    (c8, s8), (c4, s4) = pos_tables[2], pos_tables[3]
    fold = LOG2E if log2e_in_q else 1.0

    def full(a):
        return pl.BlockSpec(a.shape, lambda i: (0, 0))

    def out(i):
        return (i // nt, 0, i % nt)

    consts = (prm['norm_w'], prm['w_t'], prm['qnw'], prm['ksw'], prm['kww'], prm['qdw'], prm['kdw'])
    dts = (BF16, F32, F32, F32, BF16, BF16, F32, F32, BF16, F32)
    kern = functools.partial(_proj_kernel, qn_scale=HEAD_DIM ** -0.5 * fold, qd_scale=DIFF_QK_DIM ** -0.5 * fold)
    return pl.pallas_call(
        kern, grid=(n // tm,),
        in_specs=[pl.BlockSpec((tm, d), lambda i: (i, 0))] + [full(a) for a in consts]
        + [pl.BlockSpec((8, tm), lambda i: (0, i % nt))] * 4,
        out_specs=[pl.BlockSpec((1, r, tm), out) for r in _ROW_SIZES],
        out_shape=[jax.ShapeDtypeStruct((nb, r, t_per_batch), dt) for r, dt in zip(_ROW_SIZES, dts)],
        compiler_params=_cparams(("parallel",)), name="proj",
    )(x2, *consts, c8, s8, c4, s4)


def _compress_units(xs_ref, nu, u0, nuc, w1_ref, posl_ref, w2_ref, c):
    lhs = jnp.concatenate(
        [xs_ref[c, pl.ds(u0 * CMP_STRIDE + l, nuc, stride=CMP_STRIDE), :] for l in range(CMP_STRIDE)],
        axis=1).astype(BF16)
    return _dot(lhs, w1_ref[c])


def _compress_finish(pq, posl_ref, w1_ref, w2_ref, c):
    nu = pq.shape[0]
    pb = _dot(posl_ref[c], w1_ref[c])
    bias = pb[0:1, 0:128] + pb[1:2, 128:256]
    hid = pq[:, 0:128] + pltpu.roll(pq[:, 128:256], nu - 1, 0) + bias
    hid = hid * _sigmoid(hid)
    return _dot(hid.astype(BF16), w2_ref[c])


def _kc_norm_rope(kc, g64_ref, kcw_ref, cc_ref, sac_ref, sbc_ref):
    kc = kc * lax.rsqrt(_dot(kc * kc, g64_ref[...]) + NORM_EPS) * kcw_ref[...]
    return _rope_lanes(kc, cc_ref[...], sac_ref[...], sbc_ref[...], 8)


def _cmp_prompt_kernel(kvc_ref, w1_ref, posl_ref, w2_ref, g64_ref, kcw_ref, cc_ref, sac_ref, sbc_ref,
                       kc_ref, vct_ref, xs_ref):
    t = kvc_ref.shape[2]
    nu = t // CMP_STRIDE
    for c in range(2):
        xs_ref[c] = kvc_ref[0, c * 128:(c + 1) * 128, :].T
    outs = []
    for c in range(2):
        pq = _compress_units(xs_ref, nu, 0, nu, w1_ref, posl_ref, w2_ref, c)
        outs.append(_compress_finish(pq, posl_ref, w1_ref, w2_ref, c))
    kc = _kc_norm_rope(outs[0], g64_ref, kcw_ref, cc_ref, sac_ref, sbc_ref)
    vct = outs[1].T
    for g in range(NSA_KV_HEADS):
        kc_ref[0, g] = kc[:, g * HEAD_DIM:(g + 1) * HEAD_DIM]
        vct_ref[0, g] = vct[g * HEAD_DIM:(g + 1) * HEAD_DIM]


def _compress_prompt(kvc_t, prm, cend_tables):
    nb, _, t = kvc_t.shape
    nu = t // CMP_STRIDE
    cc, sac, sbc = cend_tables

    def full(a):
        nd = a.ndim
        return pl.BlockSpec(a.shape, lambda b: (0,) * nd)

    consts = (prm['cmp_w1'], prm['cmp_posl'], prm['cmp_w2'], prm['g64_128'], prm['kcw'], cc, sac, sbc)
    return pl.pallas_call(
        _cmp_prompt_kernel, grid=(nb,),
        in_specs=[pl.BlockSpec((1, 256, t), lambda b: (b, 0, 0))] + [full(a) for a in consts],
        out_specs=[pl.BlockSpec((1, NSA_KV_HEADS, nu, HEAD_DIM), lambda b: (b, 0, 0, 0)),
                   pl.BlockSpec((1, NSA_KV_HEADS, HEAD_DIM, nu), lambda b: (b, 0, 0, 0))],
        out_shape=[jax.ShapeDtypeStruct((nb, NSA_KV_HEADS, nu, HEAD_DIM), F32),
                   jax.ShapeDtypeStruct((nb, NSA_KV_HEADS, HEAD_DIM, nu), F32)],
        scratch_shapes=[pltpu.VMEM((2, t, LANES), F32)],
        compiler_params=_cparams(("parallel",)), name="cmp_prompt",
    )(kvc_t, *consts)


def _softmax_rows(s, mask):
    sm = jnp.where(mask, s, NEG_INF)
    m = jnp.max(sm, axis=-1, keepdims=True)
    e = jnp.where(mask, jnp.exp(sm - m), 0.0)
    return e / jnp.maximum(jnp.sum(e, axis=-1, keepdims=True), 1e-30)


def _ones_rows(t):
    return jnp.where(lax.broadcasted_iota(jnp.int32, (BF16_ROWS, t), 0) == 0, 1.0, 0.0).astype(BF16)


def _flash_init(m_scr, acc_scr):
    m_scr[...] = jnp.full(m_scr.shape, NEG_INF, F32)
    acc_scr[...] = jnp.zeros(acc_scr.shape, F32)


def _flash_step(problems, off, nk, lanes, m_scr, acc_scr):
    lo, hi = lanes
    for n, qt, k_ref, v_ref, bias_fn in problems:
        s = _dot(k_ref[pl.ds(off, nk), :], qt[:, lo:hi]) + bias_fn(off, nk)[:, lo:hi]
        m_old = m_scr[n, 0:1, lo:hi]
        m_new = jnp.maximum(m_old, jnp.max(s, axis=0, keepdims=True))
        p = jnp.exp2(s - m_new).astype(BF16)
        acc_scr[n, :, lo:hi] = jnp.exp2(m_old - m_new) * acc_scr[n, :, lo:hi] + _dot(v_ref[:, pl.ds(off, nk)], p)
        m_scr[n, :, lo:hi] = jnp.broadcast_to(m_new, (m_scr.shape[1], hi - lo))


def _flash_run(problems, c_lo, c_hi, m_scr, acc_scr):
    r = m_scr.shape[2]

    def body(c, carry):
        _flash_step(problems, pl.multiple_of(c * KEY_CHUNK, KEY_CHUNK), KEY_CHUNK, (0, r), m_scr, acc_scr)
        return carry

    lax.fori_loop(c_lo, c_hi, body, 0)


def _flash_finish(n, m_scr, acc_scr):
    dv = acc_scr.shape[1] - BF16_ROWS
    acc = acc_scr[n]
    return jnp.where(m_scr[n, 0:1] > 0.5 * NEG_INF, acc[0:dv] / jnp.maximum(acc[dv:dv + 1], 1e-30), 0.0)


def _nsa_prompt_kernel(q_ref, g_ref, kc_ref, vct_ref, kst_ref, vst_ref, kwt_ref, vwt_ref, ovt_ref, exp_ref, o_ref,
                       ks_sd, kw_sd, vs_aug, vw_aug, m_scr, acc_scr, *, tq):
    i = pl.program_id(1)
    t = vst_ref.shape[2]

    @pl.when(i == 0)
    def _():
        for kt_ref, k_sd in ((kst_ref, ks_sd), (kwt_ref, kw_sd)):
            kk = kt_ref[0].T
            for g in range(NSA_KV_HEADS):
                k_sd[g] = kk[:, g * HEAD_DIM:(g + 1) * HEAD_DIM].astype(BF16)
        for vt_ref, v_aug in ((vst_ref, vs_aug), (vwt_ref, vw_aug)):
            for g in range(NSA_KV_HEADS):
                v_aug[g, 0:HEAD_DIM] = vt_ref[0, g * HEAD_DIM:(g + 1) * HEAD_DIM, :].astype(BF16)
                v_aug[g, HEAD_DIM:HEAD_DIM + BF16_ROWS] = _ones_rows(t)

    nu = kc_ref.shape[2]
    nsel = ovt_ref.shape[0]
    assert tq == KEY_CHUNK == WINDOW
    half = tq // 2
    r = NSA_GROUP * tq

    def per_head(x):
        return jnp.concatenate([x[:, a * half:(a + 1) * half] for a in range(2) for _ in range(NSA_GROUP)], axis=1)

    def head_lanes(x, h):
        return jnp.concatenate([x[:, (a * NSA_GROUP + h) * half:(a * NSA_GROUP + h + 1) * half] for a in range(2)],
                               axis=1)

    t0 = i * tq
    tpos = t0 + lax.broadcasted_iota(jnp.int32, (1, tq), 1)
    qpos = per_head(tpos)
    cend = lax.broadcasted_iota(jnp.int32, (nu, 1), 0) * CMP_STRIDE + (CMP_BLOCK - 1)
    vis = cend <= qpos
    blk = lax.broadcasted_iota(jnp.int32, (nsel, tq), 0)
    tl = t0 + lax.broadcasted_iota(jnp.int32, (nsel, tq), 1)
    cur = tl // SEL_BLOCK
    forced = (blk == 0) | (blk == cur) | (blk == cur - 1)

    def kpos_of(off, nk):
        return off + lax.broadcasted_iota(jnp.int32, (nk, 1), 0)

    def win_mask(off, nk):
        d = tpos - kpos_of(off, nk)
        return per_head(jnp.where((d >= 0) & (d < WINDOW), 0.0, NEG_INF))

    sel_ps, win_ps, o_cmps = [], [], []
    for g in range(NSA_KV_HEADS):
        qb = q_ref[0, g * NSA_GROUP * HEAD_DIM:(g + 1) * NSA_GROUP * HEAD_DIM, :]
        qt = jnp.concatenate([qb[HEAD_DIM * h:HEAD_DIM * (h + 1), a * half:(a + 1) * half]
                              for a in range(2) for h in range(NSA_GROUP)], axis=1)

        s = _dot(kc_ref[0, g].astype(BF16), qt)
        sm = jnp.where(vis, s, NEG_INF)
        e = jnp.where(vis, jnp.exp2(sm - jnp.max(sm, axis=0, keepdims=True)), 0.0)
        p = e / jnp.maximum(jnp.sum(e, axis=0, keepdims=True), 1e-30)
        o_cmps.append(_dot(vct_ref[0, g].astype(BF16), p.astype(BF16)))

        psum = head_lanes(p, 0)
        for h in range(1, NSA_GROUP):
            psum = psum + head_lanes(p, h)
        hi = psum.astype(BF16)
        lo = (psum - hi.astype(F32)).astype(BF16)
        imp = _dot(ovt_ref[...], hi) + _dot(ovt_ref[...], lo)
        score = jnp.where(forced, FORCE_SCORE, jnp.where(blk * SEL_BLOCK <= tl, imp, NEG_INF))
        cnt = jnp.zeros((nsel, tq), jnp.int32)
        for jp in range(nsel):
            other = score[jp:jp + 1, :]
            ahead = (other > score) | ((other == score) & (blk > jp))
            cnt = cnt + jnp.where(ahead, 1, 0)
        sel_bias = jnp.where(cnt < min(N_SELECT, nsel), 0.0, NEG_INF).astype(BF16)

        def sel_mask(off, nk, sel_bias=sel_bias):
            bias = _dot(exp_ref[pl.ds(off, nk), :], sel_bias)
            return per_head(jnp.where(kpos_of(off, nk) <= tpos, bias, NEG_INF))

        sel_ps.append((2 * g, qt, ks_sd.at[g], vs_aug.at[g], sel_mask))
        win_ps.append((2 * g + 1, qt, kw_sd.at[g], vw_aug.at[g], win_mask))

    _flash_init(m_scr, acc_scr)
    c_prev = jnp.maximum(i - 1, 0)
    _flash_run(sel_ps, 0, c_prev, m_scr, acc_scr)

    def prev_chunk(c, carry):
        off = pl.multiple_of(c * KEY_CHUNK, KEY_CHUNK)
        _flash_step(sel_ps, off, KEY_CHUNK, (0, r), m_scr, acc_scr)
        _flash_step(win_ps, off, half, (0, r // 2), m_scr, acc_scr)
        _flash_step(win_ps, off + half, half, (0, r), m_scr, acc_scr)
        return carry

    lax.fori_loop(c_prev, i, prev_chunk, 0)
    off = pl.multiple_of(i * KEY_CHUNK, KEY_CHUNK)
    _flash_step(sel_ps + win_ps, off, half, (0, r), m_scr, acc_scr)
    _flash_step(sel_ps + win_ps, off + half, half, (r // 2, r), m_scr, acc_scr)

    for g in range(NSA_KV_HEADS):
        o_sel = _flash_finish(2 * g, m_scr, acc_scr)
        o_win = _flash_finish(2 * g + 1, m_scr, acc_scr)
        for h in range(NSA_GROUP):
            head = g * NSA_GROUP + h

            def gate(br):
                return g_ref[0, br * NSA_HEADS + head:br * NSA_HEADS + head + 1, :]

            o_ref[0, head * HEAD_DIM:(head + 1) * HEAD_DIM, :] = (
                head_lanes(o_cmps[g], h) * gate(0) + head_lanes(o_sel, h) * gate(1) + head_lanes(o_win, h) * gate(2)
            ).astype(BF16)


def _nsa_prompt(qn_t, g_t, kc, vct, kvs_t, kvw_t, ovt, expand_t, tq):
    nb, _, t = kvs_t.shape
    nq = t // tq
    nu = kc.shape[2]
    r = NSA_GROUP * tq
    kern = functools.partial(_nsa_prompt_kernel, tq=tq)
    kv_rows = NSA_KV_HEADS * HEAD_DIM

    def tok(b, i):
        return (b, 0, i)

    return pl.pallas_call(
        kern, grid=(nb, nq),
        in_specs=[pl.BlockSpec((1, NSA_WIDTH, tq), tok), pl.BlockSpec((1, GATE_ROWS, tq), tok),
                  pl.BlockSpec((1, NSA_KV_HEADS, nu, HEAD_DIM), lambda b, i: (b, 0, 0, 0)),
                  pl.BlockSpec((1, NSA_KV_HEADS, HEAD_DIM, nu), lambda b, i: (b, 0, 0, 0)),
                  pl.BlockSpec((1, kv_rows, t), lambda b, i: (b, 0, 0)),
                  pl.BlockSpec((1, kv_rows, t), lambda b, i: (b, 1, 0)),
                  pl.BlockSpec((1, kv_rows, t), lambda b, i: (b, 0, 0)),
                  pl.BlockSpec((1, kv_rows, t), lambda b, i: (b, 1, 0)),
                  pl.BlockSpec(ovt.shape, lambda b, i: (0, 0)),
                  pl.BlockSpec(expand_t.shape, lambda b, i: (0, 0))],
        out_specs=pl.BlockSpec((1, NSA_WIDTH, tq), tok),
        out_shape=jax.ShapeDtypeStruct((nb, NSA_WIDTH, t), BF16),
        scratch_shapes=[pltpu.VMEM((NSA_KV_HEADS, t, HEAD_DIM), BF16)] * 2
        + [pltpu.VMEM((NSA_KV_HEADS, HEAD_DIM + BF16_ROWS, t), BF16)] * 2
        + [pltpu.VMEM((2 * NSA_KV_HEADS, 8, r), F32), pltpu.VMEM((2 * NSA_KV_HEADS, HEAD_DIM + BF16_ROWS, r), F32)],
        compiler_params=_cparams(("parallel", "arbitrary")), name="nsa_prompt",
    )(qn_t, g_t, kc, vct, kvs_t, kvs_t, kvw_t, kvw_t, ovt, expand_t)


def _diff_lambda(lam_ref, lam_init):
    lq1, lk1, lq2, lk2 = lam_ref[0:1], lam_ref[1:2], lam_ref[2:3], lam_ref[3:4]
    return (jnp.exp(jnp.sum(lq1 * lk1, axis=-1, keepdims=True)) - jnp.exp(jnp.sum(lq2 * lk2, axis=-1, keepdims=True))
            + lam_init)


def _diff_prompt_kernel(q_ref, kt_ref, vt_ref, lam_ref, o_ref, k_sd, v_aug, m_scr, acc_scr, *, tq, lam_init):
    i = pl.program_id(2)
    t = kt_ref.shape[2]
    n_heads = v_aug.shape[0]

    @pl.when(i == 0)
    def _():
        kk = kt_ref[0].T
        for j in range(2 * n_heads):
            k_sd[j] = kk[:, j * DIFF_QK_DIM:(j + 1) * DIFF_QK_DIM].astype(BF16)
        for h in range(n_heads):
            v_aug[h, 0:DIFF_V_DIM] = vt_ref[0, h * DIFF_V_DIM:(h + 1) * DIFF_V_DIM, :].astype(BF16)
            v_aug[h, DIFF_V_DIM:DIFF_V_DIM + BF16_ROWS] = _ones_rows(t)

    lam = _diff_lambda(lam_ref, lam_init)
    tpos = i * tq + lax.broadcasted_iota(jnp.int32, (1, tq), 1)

    def causal(off, nk):
        kpos = off + lax.broadcasted_iota(jnp.int32, (nk, 1), 0)
        return jnp.where(kpos <= tpos, 0.0, NEG_INF)

    problems = [(j, q_ref[0, j * DIFF_QK_DIM:(j + 1) * DIFF_QK_DIM, :], k_sd.at[j], v_aug.at[j // 2], causal)
                for j in range(2 * n_heads)]
    _flash_init(m_scr, acc_scr)
    _flash_run(problems, 0, (i + 1) * (tq // KEY_CHUNK), m_scr, acc_scr)
    for h in range(n_heads):
        o_ref[0, h * DIFF_V_DIM:(h + 1) * DIFF_V_DIM, :] = (
            _flash_finish(2 * h, m_scr, acc_scr) - lam * _flash_finish(2 * h + 1, m_scr, acc_scr))


def _diff_prompt(qd_t, kd_t, vd_t, lam_vec, lam_init, tq, n_heads):
    nb, _, t = kd_t.shape
    nq = t // tq
    rows = n_heads * DIFF_V_DIM
    kern = functools.partial(_diff_prompt_kernel, tq=tq, lam_init=lam_init)
    return pl.pallas_call(
        kern, grid=(nb, DIFF_HEADS // n_heads, nq),
        in_specs=[pl.BlockSpec((1, rows, tq), lambda b, hp, i: (b, hp, i)),
                  pl.BlockSpec((1, rows, t), lambda b, hp, i: (b, hp, 0)),
                  pl.BlockSpec((1, rows, t), lambda b, hp, i: (b, hp, 0)),
                  pl.BlockSpec(lam_vec.shape, lambda b, hp, i: (0, 0))],
        out_specs=pl.BlockSpec((1, rows, tq), lambda b, hp, i: (b, hp, i)),
        out_shape=jax.ShapeDtypeStruct((nb, DIFF_WIDTH, t), F32),
        scratch_shapes=[pltpu.VMEM((2 * n_heads, t, DIFF_QK_DIM), BF16),
                        pltpu.VMEM((n_heads, DIFF_V_DIM + BF16_ROWS, t), BF16),
                        pltpu.VMEM((2 * n_heads, 8, tq), F32),
                        pltpu.VMEM((2 * n_heads, DIFF_V_DIM + BF16_ROWS, tq), F32)],
        compiler_params=_cparams(("parallel", "arbitrary", "arbitrary")), name="diff_prompt",
    )(qd_t, kd_t, vd_t, lam_vec)


def _out_kernel(x_ref, on_ref, zn_ref, od_ref, zd_ref, dow_ref, wo_ref, y_ref, *, lam_init):
    tm = x_ref.shape[0]
    o1 = on_ref[0].astype(F32) * zn_ref[0].astype(F32)
    od = od_ref[0].reshape(DIFF_HEADS, DIFF_V_DIM, tm)
    od = od * lax.rsqrt(jnp.mean(od * od, axis=1, keepdims=True) + NORM_EPS)
    od = (od * dow_ref[...].reshape(1, DIFF_V_DIM, 1) * (1.0 - lam_init)).reshape(DIFF_WIDTH, tm)
    o2 = od * zd_ref[0].astype(F32)
    o = jnp.concatenate([o1, o2], axis=0).T.astype(BF16)
    y_ref[...] = x_ref[...] + _dot(o, wo_ref[...])


def _out_proj(x2, o_nsa_t, zn_t, od_t, zd_t, prm, lam_init, tm):
    n, d = x2.shape
    t = o_nsa_t.shape[2]
    nt = t // tm

    def tok(i):
        return (i // nt, 0, i % nt)

    def full(a):
        return pl.BlockSpec(a.shape, lambda i: (0, 0))

    consts = (prm['dow_col'], prm['w_out'])
    return pl.pallas_call(
        functools.partial(_out_kernel, lam_init=lam_init), grid=(n // tm,),
        in_specs=[pl.BlockSpec((tm, d), lambda i: (i, 0))] + [pl.BlockSpec((1, 512, tm), tok)] * 4
        + [full(a) for a in consts],
        out_specs=pl.BlockSpec((tm, d), lambda i: (i, 0)), out_shape=jax.ShapeDtypeStruct((n, d), F32),
        compiler_params=_cparams(("parallel",)), name="out_proj",
    )(x2, o_nsa_t, zn_t, od_t, zd_t, *consts)


def _mix_out(x, o_nsa, zn, od, zd, g64_ref, dow_ref, wo_ref, lam_init):
    o1 = (o_nsa * zn.astype(F32)).astype(BF16)
    odn = od * lax.rsqrt(_dot(od * od, g64_ref[...]) + NORM_EPS) * dow_ref[...] * (1.0 - lam_init)
    o2 = (odn * zd.astype(F32)).astype(BF16)
    return x + _dot(o1, wo_ref[0:NSA_WIDTH]) + _dot(o2, wo_ref[NSA_WIDTH:NSA_WIDTH + DIFF_WIDTH])


def _sample_cmp_kernel(pt_ref, q_ref, *rest, pps, past, n_sel):
    pages = rest[:pps]
    (w1_ref, posl_ref, w2_ref, g64_ref, kcw_ref, cc_ref, sac_ref, sbc_ref, ov_ref, ocmp_ref, idx_ref,
     xs_ref) = rest[pps:]
    s = pl.program_id(1)
    for k in range(pps):
        row0 = pl.multiple_of((s * pps + k) * LANES, LANES)
        for c in range(2):
            xs_ref[c, pl.ds(row0, LANES), :] = pages[k][0, c * 128:(c + 1) * 128, :].T

    @pl.when(s == pl.num_programs(1) - 1)
    def _():
        nu = past // CMP_STRIDE
        nuc = min(nu, 256)
        outs = []
        for c in range(2):
            pq = jnp.concatenate([_compress_units(xs_ref, nu, u0, nuc, w1_ref, posl_ref, w2_ref, c)
                                  for u0 in range(0, nu, nuc)], axis=0)
            outs.append(_compress_finish(pq, posl_ref, w1_ref, w2_ref, c))
        kct = _kc_norm_rope(outs[0], g64_ref, kcw_ref, cc_ref, sac_ref, sbc_ref).T.astype(BF16)
        vc = outs[1].astype(BF16)
        q = q_ref[0]
        row = lax.broadcasted_iota(jnp.int32, (NSA_HEADS, 1), 0)
        first = row < NSA_GROUP
        sc = jnp.where(first, _dot(q, kct[0:HEAD_DIM]), _dot(q, kct[HEAD_DIM:2 * HEAD_DIM]))
        cend = lax.broadcasted_iota(jnp.int32, (1, nu), 1) * CMP_STRIDE + (CMP_BLOCK - 1)
        p = _softmax_rows(sc, cend <= past)
        pb = p.astype(BF16)
        ocmp_ref[0] = jnp.where(first, _dot(pb, vc[:, 0:HEAD_DIM]), _dot(pb, vc[:, HEAD_DIM:2 * HEAD_DIM]))
        psum = jnp.concatenate([jnp.sum(p[0:NSA_GROUP], axis=0, keepdims=True),
                                jnp.sum(p[NSA_GROUP:], axis=0, keepdims=True),
                                jnp.zeros((NSA_HEADS - 2, nu), F32)], axis=0)
        hi = psum.astype(BF16)
        lo = (psum - hi.astype(F32)).astype(BF16)
        imp = _dot(hi, ov_ref[...]) + _dot(lo, ov_ref[...])
        nselp = ov_ref.shape[1]
        blk = lax.broadcasted_iota(jnp.int32, (NSA_HEADS, nselp), 1)
        cur = past // SEL_BLOCK
        forced = (blk == 0) | (blk == cur) | (blk == cur - 1)
        score = jnp.where(forced, FORCE_SCORE, jnp.where(blk * SEL_BLOCK <= past, imp, NEG_INF))
        score = jnp.where(blk < n_sel, score, EXCLUDED)
        blkf = blk.astype(F32)
        lane = lax.broadcasted_iota(jnp.int32, (NSA_HEADS, LANES), 1)
        picked = jnp.zeros((NSA_HEADS, LANES), F32)
        for k in range(min(N_SELECT, n_sel)):
            mx = jnp.max(score, axis=-1, keepdims=True)
            ik = jnp.min(jnp.where(score == mx, blkf, 1e9), axis=-1, keepdims=True)
            picked = jnp.where(lane == k, ik, picked)
            score = jnp.where(blkf == ik, EXCLUDED, score)
        idx_ref[0] = picked.astype(jnp.int32)


def _sample_cmp(page_table, q8, cache_ct, prm, cend_tables, ov, past, n_sel, pps):
    db, n_pages = page_table.shape
    n_steps = n_pages // pps
    consts = (prm['cmp_w1'], prm['cmp_posl'], prm['cmp_w2'], prm['g64_128'], prm['kcw']) + tuple(cend_tables) + (ov,)

    def full(a):
        nd = a.ndim
        return pl.BlockSpec(a.shape, lambda b, s, pt: (0,) * nd)

    def page_spec(k):
        return pl.BlockSpec((1, 256, LANES), lambda b, s, pt: (pt[b, s * pps + k], 0, 0))

    kern = functools.partial(_sample_cmp_kernel, pps=pps, past=past, n_sel=n_sel)
    return pl.pallas_call(
        kern,
        grid_spec=pltpu.PrefetchScalarGridSpec(
            num_scalar_prefetch=1, grid=(db, n_steps),
            in_specs=[pl.BlockSpec((1, NSA_HEADS, HEAD_DIM), lambda b, s, pt: (b, 0, 0))]
            + [page_spec(k) for k in range(pps)] + [full(a) for a in consts],
            out_specs=[pl.BlockSpec((1, NSA_HEADS, HEAD_DIM), lambda b, s, pt: (b, 0, 0)),
                       pl.BlockSpec((1, NSA_HEADS, LANES), lambda b, s, pt: (b, 0, 0))],
            scratch_shapes=[pltpu.VMEM((2, past, LANES), F32)]),
        out_shape=[jax.ShapeDtypeStruct((db, NSA_HEADS, HEAD_DIM), F32),
                   jax.ShapeDtypeStruct((db, NSA_HEADS, LANES), jnp.int32)],
        compiler_params=_cparams(("parallel", "arbitrary")), name="sample_cmp",
    )(page_table, q8, *([cache_ct] * pps), *consts)


def _sample_attn_kernel(pt_ref, idx_ref, q_ref, knew_ref, vnew_ref, wnew_ref, win_ref, *rest, n_k, past, n_pages):
    pages = rest[:n_k]
    osel_ref, owin_ref, wout_ref = rest[n_k:]
    b = pl.program_id(0)
    g = pl.program_id(1)
    q = q_ref[0, 0]
    lane = lax.broadcasted_iota(jnp.int32, (1, LANES), 1)
    n_cached = 2 * n_pages

    ss, vts = [], []
    for k in range(n_k):
        j = idx_ref[b, g, k]
        jc = jnp.minimum(j, n_cached - 1)
        kpos = (jc // 2) * LANES + lane
        ok = (lane // SEL_BLOCK == jc % 2) & (kpos <= past) & (j < n_cached)
        ss.append(jnp.where(ok, _dot(q, pages[k][0, 0, 0].astype(BF16)), NEG_INF))
        vts.append(pages[k][0, 1, 0].astype(BF16))
    s_new = jnp.sum(q.astype(F32) * knew_ref[0, 0][0:1], axis=-1, keepdims=True)
    m = s_new
    for s in ss:
        m = jnp.maximum(m, jnp.max(s, axis=-1, keepdims=True))
    e_new = jnp.exp(s_new - m)
    l = e_new
    acc = e_new * vnew_ref[0, 0][0:1]
    for s, vt in zip(ss, vts):
        e = jnp.exp(s - m)
        l = l + jnp.sum(e, axis=-1, keepdims=True)
        acc = acc + _dot_nt(e.astype(BF16), vt)
    osel_ref[0, 0] = acc / jnp.maximum(l, 1e-30)

    w = win_ref.shape[4]
    wl = lax.broadcasted_iota(jnp.int32, (1, w), 1)
    kt = jnp.where(wl == w - 1, wnew_ref[0, 0, 0], pltpu.roll(win_ref[0, 0, 0], w - 1, 1))
    vt = jnp.where(wl == w - 1, wnew_ref[0, 1, 0], pltpu.roll(win_ref[0, 1, 0], w - 1, 1))
    wout_ref[0, 0, 0] = kt
    wout_ref[0, 1, 0] = vt
    kpos = past - (w - 1) + wl
    d = past - kpos
    p = _softmax_rows(_dot(q, kt.astype(BF16)), (d >= 0) & (d < WINDOW) & (kpos >= 0))
    owin_ref[0, 0] = _dot_nt(p.astype(BF16), vt.astype(BF16))


def _sample_attn(page_table, idx, q4, knew, vnew, wnew, win_t, cache_st, past):
    db, n_pages = page_table.shape
    n_k = idx.shape[2]
    w = win_t.shape[4]

    def per_bg(shape):
        nd = len(shape)
        return pl.BlockSpec((1, 1) + shape, lambda b, g, pt, ix: (b, g) + (0,) * nd)

    def kv_bg(last):
        return pl.BlockSpec((1, 2, 1, HEAD_DIM, last), lambda b, g, pt, ix: (b, 0, g, 0, 0))

    def page_spec(k):
        def imap(b, g, pt, ix):
            j = jnp.minimum(ix[b, g, k], 2 * n_pages - 1)
            return (pt[b, j // 2], 0, g, 0, 0)
        return pl.BlockSpec((1, 2, 1, HEAD_DIM, LANES), imap)

    kern = functools.partial(_sample_attn_kernel, n_k=n_k, past=past, n_pages=n_pages)
    o_shape = jax.ShapeDtypeStruct((db, NSA_KV_HEADS, 8, HEAD_DIM), F32)
    return pl.pallas_call(
        kern,
        grid_spec=pltpu.PrefetchScalarGridSpec(
            num_scalar_prefetch=2, grid=(db, NSA_KV_HEADS),
            in_specs=[per_bg((8, HEAD_DIM))] * 3 + [kv_bg(1), kv_bg(w)] + [page_spec(k) for k in range(n_k)],
            out_specs=[per_bg((8, HEAD_DIM)), per_bg((8, HEAD_DIM)), kv_bg(w)]),
        out_shape=[o_shape, o_shape, jax.ShapeDtypeStruct(win_t.shape, F32)],
        compiler_params=_cparams(("parallel", "arbitrary")), name="sample_attn",
    )(page_table, idx, q4, knew, vnew, wnew, win_t, *([cache_st] * n_k))


def _sample_diff_kernel(pt_ref, qbd_ref, q16_ref, k16_ref, vnew_ref, lam_ref, *rest, pps, lam_init):
    kpages = rest[:pps]
    vpages = rest[pps:2 * pps]
    o_ref, m_scr, l_scr, acc_scr = rest[2 * pps:]
    s = pl.program_id(1)
    rows = 2 * DIFF_HEADS

    @pl.when(s == 0)
    def _():
        m_scr[...] = jnp.full(m_scr.shape, NEG_INF, F32)
        l_scr[...] = jnp.zeros(l_scr.shape, F32)
        acc_scr[...] = jnp.zeros(acc_scr.shape, F32)

    qbd = qbd_ref[0]
    ss = [_dot(qbd, kp[0].astype(BF16)) for kp in kpages]
    cm = ss[0]
    for sk in ss[1:]:
        cm = jnp.maximum(cm, sk)
    m_old = m_scr[...]
    m_new = jnp.maximum(m_old, jnp.max(cm, axis=-1, keepdims=True))
    alpha = jnp.exp(m_old - m_new)
    lsum = jnp.zeros((rows, LANES), F32)
    accs = jnp.zeros(acc_scr.shape, F32)
    for sk, vp in zip(ss, vpages):
        e = jnp.exp(sk - m_new)
        lsum = lsum + e
        accs = accs + _dot_nt(e.astype(BF16), vp[0].astype(BF16))
    m_scr[...] = m_new
    l_scr[...] = alpha * l_scr[...] + jnp.sum(lsum, axis=-1, keepdims=True)
    acc_scr[...] = alpha[:, 0:1] * acc_scr[...] + accs

    @pl.when(s == pl.num_programs(1) - 1)
    def _():
        s_new = jnp.sum(q16_ref[0] * k16_ref[0], axis=-1, keepdims=True)
        m1 = m_scr[...][:, 0:1]
        m2 = jnp.maximum(m1, s_new)
        a2 = jnp.exp(m1 - m2)
        e_new = jnp.exp(s_new - m2)
        l = a2 * l_scr[...][:, 0:1] + e_new
        acc = acc_scr[...]
        col = lax.broadcasted_iota(jnp.int32, acc.shape, 1) // DIFF_V_DIM
        rowh = lax.broadcasted_iota(jnp.int32, acc.shape, 0) % DIFF_HEADS
        acc = jnp.where(col == rowh, acc, 0.0)
        o16 = acc[:, 0:DIFF_V_DIM]
        for h in range(1, DIFF_HEADS):
            o16 = o16 + acc[:, h * DIFF_V_DIM:(h + 1) * DIFF_V_DIM]
        vnew = vnew_ref[0]
        o16 = (a2 * o16 + e_new * jnp.concatenate([vnew, vnew], axis=0)) / jnp.maximum(l, 1e-30)
        o_ref[0] = o16[0:DIFF_HEADS] - _diff_lambda(lam_ref, lam_init) * o16[DIFF_HEADS:]


def _sample_diff(page_table, qbd, q16, k16, vnew, lam_vec, cache_dkt, cache_dvt, lam_init, pps):
    db, n_pages = page_table.shape
    n_steps = n_pages // pps
    rows = 2 * DIFF_HEADS

    def per_b(shape):
        nd = len(shape)
        return pl.BlockSpec((1,) + shape, lambda b, s, pt: (b,) + (0,) * nd)

    def page_spec(k):
        return pl.BlockSpec((1, 512, LANES), lambda b, s, pt: (pt[b, s * pps + k], 0, 0))

    kern = functools.partial(_sample_diff_kernel, pps=pps, lam_init=lam_init)
    return pl.pallas_call(
        kern,
        grid_spec=pltpu.PrefetchScalarGridSpec(
            num_scalar_prefetch=1, grid=(db, n_steps),
            in_specs=[per_b((rows, 512)), per_b((rows, DIFF_QK_DIM)), per_b((rows, DIFF_QK_DIM)),
                      per_b((DIFF_HEADS, DIFF_V_DIM)), pl.BlockSpec(lam_vec.shape, lambda b, s, pt: (0, 0))]
            + [page_spec(k) for k in range(pps)] * 2,
            out_specs=per_b((DIFF_HEADS, DIFF_V_DIM)),
            scratch_shapes=[pltpu.VMEM((rows, LANES), F32), pltpu.VMEM((rows, LANES), F32),
                            pltpu.VMEM((rows, 512), F32)]),
        out_shape=jax.ShapeDtypeStruct((db, DIFF_HEADS, DIFF_V_DIM), F32),
        compiler_params=_cparams(("parallel", "arbitrary")), name="sample_diff",
    )(page_table, qbd, q16, k16, vnew, lam_vec, *([cache_dkt] * pps), *([cache_dvt] * pps))


def _sample_out_kernel(x_ref, oc_ref, os_ref, ow_ref, g0_ref, g1_ref, g2_ref, zn_ref, od_ref, zd_ref, g64_ref, dow_ref,
                       wo_ref, y_ref, *, lam_init):
    o_nsa = oc_ref[...] * g0_ref[...] + os_ref[...] * g1_ref[...] + ow_ref[...] * g2_ref[...]
    y_ref[...] = _mix_out(x_ref[...], o_nsa, zn_ref[...], od_ref[...], zd_ref[...], g64_ref, dow_ref, wo_ref, lam_init)


def _sample_out(x2, branches, gates, zn, od, zd, prm, lam_init):
    args = (x2,) + tuple(branches) + tuple(gates) + (zn, od, zd, prm['g64'], prm['dow'], prm['w_out'])
    return pl.pallas_call(
        functools.partial(_sample_out_kernel, lam_init=lam_init),
        out_shape=jax.ShapeDtypeStruct(x2.shape, F32), compiler_params=_cparams(None), name="sample_out",
    )(*args)


def _layer_params(lp):
    w = lp['w_in']
    d = w.shape[0]
    cols = [w[:, _OFFS[k]:_OFFS[k + 1]] for k in _ROW_ORDER]
    cols.append(jnp.zeros((d, GATE_ROWS - _SIZES[4]), F32))
    w_t = jnp.concatenate(cols, axis=1).T

    def cmp_w1(w1):
        w1 = w1.reshape(2, CMP_STRIDE, HEAD_DIM, HEAD_DIM)
        eye = jnp.eye(NSA_KV_HEADS, dtype=F32)
        full = jnp.einsum('hlde,gk->lgdhke', w1, eye)
        return full.reshape(CMP_STRIDE * NSA_KV_HEADS * HEAD_DIM, 2 * NSA_KV_HEADS * HEAD_DIM)

    def cmp_pos(pos):
        pos = pos.reshape(2, CMP_STRIDE, 1, HEAD_DIM)
        rows = jnp.broadcast_to(pos, (2, CMP_STRIDE, NSA_KV_HEADS, HEAD_DIM)).reshape(2, -1)
        return jnp.concatenate([rows, jnp.zeros((6, rows.shape[1]), F32)], axis=0)

    def cmp_w2(w2):
        return jnp.kron(jnp.eye(NSA_KV_HEADS, dtype=F32), w2)

    lam_vec = jnp.concatenate([lp['lambda_q1'][None], lp['lambda_k1'][None], lp['lambda_q2'][None],
                               lp['lambda_k2'][None], jnp.zeros((4, DIFF_QK_DIM), F32)], axis=0)
    return {
        'norm_w': lp['norm_w'][None], 'w_t': w_t.astype(BF16),
        'g64': _group_mean_matrix(512, HEAD_DIM), 'g64_128': _group_mean_matrix(LANES, HEAD_DIM),
        'qnw': lp['nsa_q_norm'][:, None], 'qdw': lp['diff_q_norm'][:, None],
        'ksw': lp['nsa_ks_norm'][:, None], 'kww': lp['nsa_kw_norm'][:, None], 'kdw': lp['diff_k_norm'][:, None],
        'kcw': jnp.tile(lp['nsa_kc_norm'], NSA_KV_HEADS)[None],
        'cmp_w1': jnp.stack([cmp_w1(lp['cmp_w1_k']), cmp_w1(lp['cmp_w1_v'])]).astype(BF16),
        'cmp_posl': jnp.stack([cmp_pos(lp['cmp_pos_k']), cmp_pos(lp['cmp_pos_v'])]).astype(BF16),
        'cmp_w2': jnp.stack([cmp_w2(lp['cmp_w2_k']), cmp_w2(lp['cmp_w2_v'])]).astype(BF16),
        'dow': jnp.tile(lp['diff_out_norm'], DIFF_HEADS)[None], 'dow_col': lp['diff_out_norm'][:, None],
        'w_out': lp['w_out'].astype(BF16), 'lam_vec': lam_vec,
    }


def _overlap_t(n_sel_pad, nu, n_cmp, n_sel):
    cs = np.arange(nu)[None, :] * CMP_STRIDE
    ss = np.arange(n_sel_pad)[:, None] * SEL_BLOCK
    ov = np.clip(np.minimum(cs + CMP_BLOCK, ss + SEL_BLOCK) - np.maximum(cs, ss), 0, None) / CMP_STRIDE
    ov = ov * (np.arange(nu)[None, :] < n_cmp) * (np.arange(n_sel_pad)[:, None] < n_sel)
    return jnp.asarray(ov, dtype=BF16)


def _prompt_layer(x, prm, lam_init):
    b, t, d = x.shape
    x2 = x.reshape(b * t, d)
    tables = _rope_tables(np.arange(t))
    qn_t, kvc_t, kvs_t, kvw_t, zn_t, qd_t, kd_t, vd_t, zd_t, g_t = _project(x2, tables, t, prm, 512, True)
    nu = t // CMP_STRIDE
    n_cmp = nu - CMP_BLOCK // CMP_STRIDE + 1
    cend_tables = _rope_tables(np.arange(nu) * CMP_STRIDE + CMP_BLOCK - 1)[0]
    kc, vct = _compress_prompt(kvc_t, prm, cend_tables)
    n_sel = -(-t // SEL_BLOCK)
    ovt = _overlap_t(n_sel, nu, n_cmp, n_sel)
    expand_t = jnp.asarray(np.arange(t)[:, None] // SEL_BLOCK == np.arange(n_sel)[None, :], dtype=BF16)
    o_nsa_t = _nsa_prompt(qn_t, g_t, kc, vct, kvs_t, kvw_t, ovt, expand_t, 512)
    o_d_t = _diff_prompt(qd_t, kd_t, vd_t, prm['lam_vec'], lam_init, 512, 8)
    y = _out_proj(x2, o_nsa_t, zn_t, o_d_t, zd_t, prm, lam_init, 512).reshape(b, t, d)

    def state(a_t, shape_tail):
        return jnp.transpose(a_t.reshape((b,) + shape_tail + (a_t.shape[-1],)),
                             (0, len(shape_tail) + 1) + tuple(range(1, len(shape_tail) + 1)))

    kv_tail = (2, NSA_KV_HEADS, HEAD_DIM)
    w = min(WINDOW, t)
    return y, (state(kvc_t, kv_tail), state(kvs_t, kv_tail), state(kvw_t[:, :, t - w:], kv_tail),
               state(kd_t, (DIFF_HEADS, 2, DIFF_QK_DIM)), state(vd_t, (DIFF_HEADS, DIFF_V_DIM)))


def _sample_layer(x, cache_c, cache_s, cache_dk, cache_dv, win, page_table, prm, lam_init):
    db, t, d = x.shape
    assert t == 1 and db % 8 == 0
    n_pool, page = cache_c.shape[:2]
    assert page == LANES
    n_pages = page_table.shape[1]
    past = n_pages * page
    x2 = x.reshape(db, d)
    dbp = -(-db // LANES) * LANES
    tables = _rope_tables(np.full((dbp,), past))
    proj = _project(jnp.pad(x2, ((0, dbp - db), (0, 0))), tables, dbp, prm, dbp, False)
    qn, kvc, kvs, kvw, zn, qd, kd, vd, zd, gn = [a[0, :, :db].T for a in proj]

    cache_ct = jnp.transpose(cache_c, (0, 2, 3, 4, 1)).reshape(n_pool, 256, page)
    cache_st = jnp.transpose(cache_s, (0, 2, 3, 4, 1))
    cache_dkt = jnp.transpose(cache_dk, (0, 2, 3, 4, 1)).reshape(n_pool, 512, page)
    cache_dvt = jnp.transpose(cache_dv, (0, 2, 3, 1)).reshape(n_pool, 512, page)
    win_t = jnp.transpose(win, (0, 2, 3, 4, 1))

    nu = past // CMP_STRIDE
    n_cmp = nu - CMP_BLOCK // CMP_STRIDE + 1
    n_sel = -(-(past + t) // SEL_BLOCK)
    n_sel_pad = -(-n_sel // LANES) * LANES
    ov = _overlap_t(n_sel_pad, nu, n_cmp, n_sel).T
    cend_tables = _rope_tables(np.arange(nu) * CMP_STRIDE + CMP_BLOCK - 1)[0]
    pps = min(64, n_pages)
    o_cmp, picked = _sample_cmp(page_table, qn.reshape(db, NSA_HEADS, HEAD_DIM), cache_ct, prm, cend_tables, ov, past,
                                n_sel, pps)
    n_k = min(N_SELECT, n_sel)
    idx = picked[:, :NSA_KV_HEADS, :n_k]

    def rows8(a):
        return jnp.pad(a, ((0, 0), (0, 0), (0, 8 - a.shape[2]), (0, 0)))

    kvs5 = kvs.reshape(db, 2, NSA_KV_HEADS, 1, HEAD_DIM)
    q4 = rows8(qn.reshape(db, NSA_KV_HEADS, NSA_GROUP, HEAD_DIM))
    o_sel, o_win, win_new = _sample_attn(page_table, idx, q4, rows8(kvs5[:, 0]), rows8(kvs5[:, 1]),
                                         kvw.reshape(db, 2, NSA_KV_HEADS, HEAD_DIM, 1), win_t, cache_st, past)

    qd3 = qd.astype(F32).reshape(db, DIFF_HEADS, 2, DIFF_QK_DIM)
    q16 = jnp.transpose(qd3, (0, 2, 1, 3)).reshape(db, 2 * DIFF_HEADS, DIFF_QK_DIM)
    own = np.zeros((2 * DIFF_HEADS, DIFF_QK_COLS), np.float32)
    for c in range(2):
        for h in range(DIFF_HEADS):
            lo = h * 2 * DIFF_QK_DIM + c * DIFF_QK_DIM
            own[c * DIFF_HEADS + h, lo:lo + DIFF_QK_DIM] = 1.0
    qbd = (qd.astype(F32)[:, None, :] * own[None]).astype(BF16)
    kd3 = kd.reshape(db, DIFF_HEADS, 2, DIFF_QK_DIM)
    k16 = jnp.transpose(kd3, (0, 2, 1, 3)).reshape(db, 2 * DIFF_HEADS, DIFF_QK_DIM)
    vnew = vd.reshape(db, DIFF_HEADS, DIFF_V_DIM)
    o_d = _sample_diff(page_table, qbd, q16, k16, vnew, prm['lam_vec'], cache_dkt, cache_dvt, lam_init,
                       min(32, n_pages))

    gates = [jnp.repeat(gn[:, br * NSA_HEADS:(br + 1) * NSA_HEADS], HEAD_DIM, axis=1) for br in range(3)]
    branches = (o_cmp.reshape(db, NSA_WIDTH), o_sel[:, :, :NSA_GROUP].reshape(db, NSA_WIDTH),
                o_win[:, :, :NSA_GROUP].reshape(db, NSA_WIDTH))
    y = _sample_out(x2, branches, gates, zn, o_d.reshape(db, DIFF_WIDTH), zd, prm, lam_init).reshape(db, t, d)

    kv_shape = (db, t, 2, NSA_KV_HEADS, HEAD_DIM)
    states = (kvc.reshape(kv_shape), kvs.reshape(kv_shape), jnp.transpose(win_new, (0, 4, 1, 2, 3)),
              kd3.reshape(db, t, DIFF_HEADS, 2, DIFF_QK_DIM), vnew.reshape(db, t, DIFF_HEADS, DIFF_V_DIM))
    return y, states


def kernel(x_prompt, x_sample, cache_nsa_cmp_kv, cache_nsa_sel_kv, cache_diff_k, cache_diff_v, state_nsa_win_kv, page_table, norm_w, w_in, nsa_q_norm, nsa_kc_norm, nsa_ks_norm, nsa_kw_norm, cmp_pos_k, cmp_w1_k, cmp_w2_k, cmp_pos_v, cmp_w1_v, cmp_w2_v, diff_q_norm, diff_k_norm, lambda_q1, lambda_k1, lambda_q2, lambda_k2, diff_out_norm, w_out):
    depth = w_in.shape[0]
    y_p, y_s = x_prompt, x_sample
    p_states, s_states = [], []
    for layer in range(depth):
        lp = {
            'norm_w': norm_w[layer], 'w_in': w_in[layer],
            'nsa_q_norm': nsa_q_norm[layer], 'nsa_kc_norm': nsa_kc_norm[layer],
            'nsa_ks_norm': nsa_ks_norm[layer], 'nsa_kw_norm': nsa_kw_norm[layer],
            'cmp_pos_k': cmp_pos_k[layer], 'cmp_w1_k': cmp_w1_k[layer], 'cmp_w2_k': cmp_w2_k[layer],
            'cmp_pos_v': cmp_pos_v[layer], 'cmp_w1_v': cmp_w1_v[layer], 'cmp_w2_v': cmp_w2_v[layer],
            'diff_q_norm': diff_q_norm[layer], 'diff_k_norm': diff_k_norm[layer],
            'lambda_q1': lambda_q1[layer], 'lambda_k1': lambda_k1[layer],
            'lambda_q2': lambda_q2[layer], 'lambda_k2': lambda_k2[layer],
            'diff_out_norm': diff_out_norm[layer], 'w_out': w_out[layer],
        }
        lam_init = 0.8 - 0.6 * math.exp(-0.3 * layer)
        prm = _layer_params(lp)
        y_p, ps = _prompt_layer(y_p, prm, lam_init)
        y_s, ss = _sample_layer(y_s, cache_nsa_cmp_kv[layer], cache_nsa_sel_kv[layer], cache_diff_k[layer],
                                cache_diff_v[layer], state_nsa_win_kv[layer], page_table, prm, lam_init)
        p_states.append(ps)
        s_states.append(ss)
    p_c, p_s, p_w, p_dk, p_dv = [jnp.stack(t, axis=0) for t in zip(*p_states)]
    s_c, s_s, s_w, s_dk, s_dv = [jnp.stack(t, axis=0) for t in zip(*s_states)]
    return (y_p, y_s, p_c, p_s, p_w, p_dk, p_dv, s_c, s_s, s_w, s_dk, s_dv)
```
